```python
import math
import jax, jax.numpy as jnp
from jax import lax
import numpy as np

D_MODEL = 1024
BATCH = 16
SEQ = 4096
DEPTH = 1

D_MIX = D_MODEL
D_CONV = D_MIX // 2
D_SSM = D_MIX - D_CONV
CONV_HEADS = 8
CONV_HEAD_DIM = D_CONV // CONV_HEADS
CONV_WIDTH = 31
SSM_GROUP = 16
SSM_GROUPS = D_SSM // SSM_GROUP
SSM_STATE = 64
D_FF = 128 * ((8 * D_MODEL // 3 + 127) // 128)
D_IN = 2 * D_CONV + D_SSM
FFN_RES = 0.5
EPS = 1e-6

kernel_name = "macaron_conv_s5_hybrid_layer"


def rmsnorm(x, g):
    xf = x.astype(jnp.float32)
    xf = xf * lax.rsqrt(jnp.mean(xf * xf, axis=-1, keepdims=True) + EPS)
    return (xf * g.astype(jnp.float32)).astype(x.dtype)


def layernorm(x, g, b):
    xf = x.astype(jnp.float32)
    mu = jnp.mean(xf, axis=-1, keepdims=True)
    xc = xf - mu
    var = jnp.mean(xc * xc, axis=-1, keepdims=True)
    y = xc * lax.rsqrt(var + EPS) * g.astype(jnp.float32) + b.astype(jnp.float32)
    return y.astype(x.dtype)


def swiglu(h, w1, w3, w2):
    return (jax.nn.silu(h @ w1) * (h @ w3)) @ w2


def conv_module(a_val, a_gate, conv_w, conv_b, ln_g, ln_b):
    a = a_val * jax.nn.sigmoid(a_gate)
    a = lax.conv_general_dilated(
        a, conv_w[:, None, :].astype(a.dtype),
        window_strides=(1,), padding=[(CONV_WIDTH - 1, 0)],
        dimension_numbers=("NWC", "WIO", "NWC"),
        feature_group_count=D_CONV) + conv_b
    a = layernorm(a, ln_g, ln_b)
    return jax.nn.silu(a)


def _complex_affine_combine(e1, e2):
    a1r, a1i, b1r, b1i = e1
    a2r, a2i, b2r, b2i = e2
    ar = a2r * a1r - a2i * a1i
    ai = a2r * a1i + a2i * a1r
    br = a2r * b1r - a2i * b1i + b2r
    bi = a2r * b1i + a2i * b1r + b2i
    return (ar, ai, br, bi)


def s5_layer(u, A_re, A_im, log_dt, B_re, B_im, C_re, C_im, D_skip, glu_w, glu_b):
    bsz, seq = u.shape[0], u.shape[1]
    ug = u.astype(jnp.float32).reshape(bsz, seq, SSM_GROUPS, SSM_GROUP)
    dt = jnp.exp(log_dt.astype(jnp.float32))[:, None]
    lr = A_re.astype(jnp.float32)
    li = A_im.astype(jnp.float32)
    zr, zi = lr * dt, li * dt
    mag = jnp.exp(zr)
    abar_r, abar_i = mag * jnp.cos(zi), mag * jnp.sin(zi)
    den = lr * lr + li * li
    nr = abar_r - 1.0
    coef_r = (nr * lr + abar_i * li) / den
    coef_i = (abar_i * lr - nr * li) / den
    br_, bi_ = B_re.astype(jnp.float32), B_im.astype(jnp.float32)
    bb_r = coef_r[..., None] * br_ - coef_i[..., None] * bi_
    bb_i = coef_r[..., None] * bi_ + coef_i[..., None] * br_
    bu_r = jnp.einsum("bsgh,gph->bsgp", ug, bb_r)
    bu_i = jnp.einsum("bsgh,gph->bsgp", ug, bb_i)
    ar_all = jnp.broadcast_to(abar_r, bu_r.shape)
    ai_all = jnp.broadcast_to(abar_i, bu_r.shape)
    _, _, xr, xi = lax.associative_scan(
        _complex_affine_combine, (ar_all, ai_all, bu_r, bu_i), axis=1)
    y = (jnp.einsum("bsgp,ghp->bsgh", xr, C_re.astype(jnp.float32))
         - jnp.einsum("bsgp,ghp->bsgh", xi, C_im.astype(jnp.float32)))
    y = y + D_skip.astype(jnp.float32).reshape(SSM_GROUPS, SSM_GROUP) * ug
    y = y.reshape(bsz, seq, D_SSM).astype(u.dtype)
    y = jax.nn.gelu(y)
    return y * jax.nn.sigmoid(y @ glu_w + glu_b)


def setup_inputs(seed: int = 0) -> dict:
    key = jax.random.key(seed)
    ks = iter(jax.random.split(key, 48))
    L = DEPTH
    f32 = jnp.float32

    def nrm(shape, scale):
        return scale * jax.random.normal(next(ks), shape, f32)

    def gain(shape):
        return 1.0 + nrm(shape, 0.02)

    n_idx = jnp.arange(SSM_STATE, dtype=f32)
    A_re = -0.5 + nrm((L, SSM_GROUPS, SSM_STATE), 0.01)
    A_im = jnp.pi * n_idx[None, None, :] + nrm((L, SSM_GROUPS, SSM_STATE), 0.01)
    log_dt = jax.random.uniform(next(ks), (L, SSM_GROUPS), f32,
                                minval=math.log(1e-3), maxval=math.log(1e-1))
    b_scale = (SSM_GROUP ** -0.5) / math.sqrt(2.0)
    c_scale = (SSM_STATE ** -0.5) / math.sqrt(2.0)
    return {
        "x": nrm((BATCH, SEQ, D_MODEL), 1.0),
        "norm_ffn1": gain((L, D_MODEL)),
        "ffn1_w1": nrm((L, D_MODEL, D_FF), D_MODEL ** -0.5),
        "ffn1_w3": nrm((L, D_MODEL, D_FF), D_MODEL ** -0.5),
        "ffn1_w2": nrm((L, D_FF, D_MODEL), D_FF ** -0.5),
        "norm_mix": gain((L, D_MODEL)),
        "w_in": nrm((L, D_MODEL, D_IN), D_MODEL ** -0.5),
        "conv_w": nrm((L, CONV_WIDTH, D_CONV), CONV_WIDTH ** -0.5),
        "conv_b": nrm((L, D_CONV), 0.02),
        "conv_ln_g": gain((L, D_CONV)),
        "conv_ln_b": nrm((L, D_CONV), 0.02),
        "conv_out_g": gain((L, D_CONV)),
        "ssm_A_re": A_re,
        "ssm_A_im": A_im,
        "ssm_log_dt": log_dt,
        "ssm_B_re": nrm((L, SSM_GROUPS, SSM_STATE, SSM_GROUP), b_scale),
        "ssm_B_im": nrm((L, SSM_GROUPS, SSM_STATE, SSM_GROUP), b_scale),
        "ssm_C_re": nrm((L, SSM_GROUPS, SSM_GROUP, SSM_STATE), c_scale),
        "ssm_C_im": nrm((L, SSM_GROUPS, SSM_GROUP, SSM_STATE), c_scale),
        "ssm_D": 1.0 + nrm((L, D_SSM), 0.1),
        "ssm_glu_w": nrm((L, D_SSM, D_SSM), D_SSM ** -0.5),
        "ssm_glu_b": nrm((L, D_SSM), 0.02),
        "ssm_out_g": gain((L, D_SSM)),
        "w_out": nrm((L, D_MIX, D_MODEL), D_MIX ** -0.5),
        "norm_ffn2": gain((L, D_MODEL)),
        "ffn2_w1": nrm((L, D_MODEL, D_FF), D_MODEL ** -0.5),
        "ffn2_w3": nrm((L, D_MODEL, D_FF), D_MODEL ** -0.5),
        "ffn2_w2": nrm((L, D_FF, D_MODEL), D_FF ** -0.5),
        "norm_final": gain((D_MODEL,)),
    }


def reference(x, norm_ffn1, ffn1_w1, ffn1_w3, ffn1_w2, norm_mix, w_in,
              conv_w, conv_b, conv_ln_g, conv_ln_b, conv_out_g,
              ssm_A_re, ssm_A_im, ssm_log_dt, ssm_B_re, ssm_B_im, ssm_C_re, ssm_C_im,
              ssm_D, ssm_glu_w, ssm_glu_b, ssm_out_g, w_out,
              norm_ffn2, ffn2_w1, ffn2_w3, ffn2_w2, norm_final):
    for l in range(DEPTH):
        x = x + FFN_RES * swiglu(rmsnorm(x, norm_ffn1[l]), ffn1_w1[l], ffn1_w3[l], ffn1_w2[l])

        h = rmsnorm(x, norm_mix[l])
        proj = h @ w_in[l]
        a_val = proj[..., :D_CONV]
        a_gate = proj[..., D_CONV:2 * D_CONV]
        u = proj[..., 2 * D_CONV:]

        a = conv_module(a_val, a_gate, conv_w[l], conv_b[l], conv_ln_g[l], conv_ln_b[l])
        a = rmsnorm(a, conv_out_g[l])

        s = s5_layer(u, ssm_A_re[l], ssm_A_im[l], ssm_log_dt[l], ssm_B_re[l], ssm_B_im[l],
                     ssm_C_re[l], ssm_C_im[l], ssm_D[l], ssm_glu_w[l], ssm_glu_b[l])
        s = rmsnorm(s, ssm_out_g[l])

        mixed = jnp.concatenate([a, s], axis=-1)
        x = x + mixed @ w_out[l]

        x = x + FFN_RES * swiglu(rmsnorm(x, norm_ffn2[l]), ffn2_w1[l], ffn2_w3[l], ffn2_w2[l])
    return rmsnorm(x, norm_final)
```

```python
import functools

import jax
import jax.numpy as jnp
from jax import lax
from jax.experimental import pallas as pl
from jax.experimental.pallas import tpu as pltpu

D_MODEL = 1024
D_CONV = 512
D_SSM = 512
CONV_WIDTH = 31
SSM_GROUP = 16
SSM_GROUPS = 32
SSM_STATE = 64
D_FF = 2816
D_IN = 2 * D_CONV + D_SSM
FFN_RES = 0.5
EPS = 1e-6

LANES = 128
SLAB_GROUPS = LANES // SSM_GROUP
N_SLAB = D_SSM // LANES
SLAB_STATE = SLAB_GROUPS * SSM_STATE
CHUNK = 8
CHUNK_K = CHUNK * LANES
HALO = 32

VMEM_LIMIT = 56 * 1024 * 1024

F32 = jnp.float32
BF16 = jnp.bfloat16


def _rmsnorm(x, g):
    ms = jnp.mean(x * x, axis=-1, keepdims=True)
    return x * lax.rsqrt(ms + EPS) * g


def _swiglu_residual(x, g, w1_ref, w3_ref, w2_ref):
    h = _rmsnorm(x, g).astype(BF16)
    a = jnp.dot(h, w1_ref[...], preferred_element_type=F32)
    b = jnp.dot(h, w3_ref[...], preferred_element_type=F32)
    z = (a * jax.nn.sigmoid(a) * b).astype(BF16)
    o = jnp.dot(z, w2_ref[...], preferred_element_type=F32)
    return x + FFN_RES * o


def _const_spec(shape):
    nd = len(shape)
    return pl.BlockSpec(shape, lambda *_: (0,) * nd, pipeline_mode=pl.Buffered(1))


def _ffn1_proj_kernel(x_ref, g1_ref, w1_ref, w3_ref, w2_ref, gm_ref, win_ref,
                      x1_ref, a_ref, u_ref):
    x1 = _swiglu_residual(x_ref[...], g1_ref[...], w1_ref, w3_ref, w2_ref)
    x1_ref[...] = x1
    h2 = _rmsnorm(x1, gm_ref[...]).astype(BF16)
    proj = jnp.dot(h2, win_ref[...], preferred_element_type=F32)
    a_val = proj[:, :D_CONV]
    a_gate = proj[:, D_CONV:2 * D_CONV]
    a_ref[...] = a_val * jax.nn.sigmoid(a_gate)
    for s in range(N_SLAB):
        lo = 2 * D_CONV + s * LANES
        u_ref[s] = proj[:, lo:lo + LANES]


def _ffn1_proj(x2d, g1, w1, w3, w2, gm, win, tm):
    t = x2d.shape[0]
    row = lambda i: (i, 0)
    return pl.pallas_call(
        _ffn1_proj_kernel,
        grid=(t // tm,),
        in_specs=[
            pl.BlockSpec((tm, D_MODEL), row),
            _const_spec((1, D_MODEL)),
            _const_spec((D_MODEL, D_FF)),
            _const_spec((D_MODEL, D_FF)),
            _const_spec((D_FF, D_MODEL)),
            _const_spec((1, D_MODEL)),
            _const_spec((D_MODEL, D_IN)),
        ],
        out_specs=[
            pl.BlockSpec((tm, D_MODEL), row),
            pl.BlockSpec((tm, D_CONV), row),
            pl.BlockSpec((N_SLAB, tm, LANES), lambda i: (0, i, 0)),
        ],
        out_shape=[
            jax.ShapeDtypeStruct((t, D_MODEL), F32),
            jax.ShapeDtypeStruct((t, D_CONV), F32),
            jax.ShapeDtypeStruct((N_SLAB, t, LANES), F32),
        ],
        compiler_params=pltpu.CompilerParams(
            dimension_semantics=("parallel",), vmem_limit_bytes=VMEM_LIMIT),
        name="ffn1_proj",
    )(x2d, g1, w1, w3, w2, gm, win)


def _s5_weights(a_re, a_im, log_dt, b_re, b_im, c_re, c_im, d_skip):
    hp = lax.Precision.HIGHEST
    L = CHUNK
    dt = jnp.exp(log_dt)[:, None]
    zr, zi = a_re * dt, a_im * dt
    mag = jnp.exp(zr)
    abar_r, abar_i = mag * jnp.cos(zi), mag * jnp.sin(zi)
    den = a_re * a_re + a_im * a_im
    nr = abar_r - 1.0
    coef_r = (nr * a_re + abar_i * a_im) / den
    coef_i = (abar_i * a_re - nr * a_im) / den
    bb_r = coef_r[..., None] * b_re - coef_i[..., None] * b_im
    bb_i = coef_r[..., None] * b_im + coef_i[..., None] * b_re
    n = jnp.arange(L + 1, dtype=F32)[:, None, None]
    pmag = jnp.exp(n * zr)
    pw_r, pw_i = pmag * jnp.cos(n * zi), pmag * jnp.sin(n * zi)
    cp_r = c_re[None] * pw_r[:, :, None, :] - c_im[None] * pw_i[:, :, None, :]
    cp_i = c_re[None] * pw_i[:, :, None, :] + c_im[None] * pw_r[:, :, None, :]
    kern = (jnp.einsum("ngop,gph->ngoh", cp_r[:L], bb_r, precision=hp)
            - jnp.einsum("ngop,gph->ngoh", cp_i[:L], bb_i, precision=hp))
    d_diag = d_skip.reshape(SSM_GROUPS, SSM_GROUP)[:, :, None] * jnp.eye(SSM_GROUP, dtype=F32)
    kern = kern.at[0].add(d_diag)
    lag = jnp.arange(L)[None, :] - jnp.arange(L)[:, None]
    kst = jnp.where((lag >= 0)[:, :, None, None, None], kern[jnp.clip(lag, 0, L - 1)], 0.0)
    eye = jnp.eye(SLAB_GROUPS, dtype=F32)
    kst = kst.reshape(L, L, N_SLAB, SLAB_GROUPS, SSM_GROUP, SSM_GROUP)
    toep = jnp.einsum("stSioh,ij->Ssihtjo", kst, eye).reshape(N_SLAB, L * LANES, L * LANES)
    vr = cp_r[1:].reshape(L, N_SLAB, SLAB_GROUPS, SSM_GROUP, SSM_STATE)
    vi = -cp_i[1:].reshape(L, N_SLAB, SLAB_GROUPS, SSM_GROUP, SSM_STATE)
    vmat = jnp.stack([vr, vi], axis=0)
    vmat = jnp.einsum("ctSiop,ij->Sciptjo", vmat, eye).reshape(N_SLAB, 2 * SLAB_STATE, L * LANES)
    tv = jnp.concatenate([toep, vmat], axis=1).astype(BF16)
    nrev = (L - 1) - n[:L]
    rmag = jnp.exp(nrev * zr)
    rv_r, rv_i = rmag * jnp.cos(nrev * zi), rmag * jnp.sin(nrev * zi)
    wr = rv_r[..., None] * bb_r[None] - rv_i[..., None] * bb_i[None]
    wi = rv_r[..., None] * bb_i[None] + rv_i[..., None] * bb_r[None]
    wmat = jnp.stack([wr, wi], axis=0).reshape(2, L, N_SLAB, SLAB_GROUPS, SSM_STATE, SSM_GROUP)
    wmat = jnp.einsum("csSiph,ij->Ssihcjp", wmat, eye).reshape(N_SLAB, L * LANES, 2 * SLAB_STATE)
    al = jnp.stack([pw_r[L].reshape(N_SLAB, SLAB_STATE), pw_i[L].reshape(N_SLAB, SLAB_STATE)], axis=1)
    return wmat.astype(BF16), tv, al


def _s5_kernel(u_ref, w_ref, tv_ref, al_ref, y_ref, lhs_ref, z_ref, xp_ref, st_ref, *, nb, mc):
    n_state_slab = 2 * SLAB_STATE // LANES

    @pl.when(pl.program_id(1) == 0)
    def _():
        st_ref[...] = jnp.zeros_like(st_ref)

    for b in range(nb):
        for s in range(CHUNK):
            piece = u_ref[b, pl.ds(s, mc, stride=CHUNK), :]
            lhs_ref[b * mc:(b + 1) * mc, s * LANES:(s + 1) * LANES] = piece.astype(BF16)

    z = jnp.dot(lhs_ref[:, :CHUNK_K], w_ref[...], preferred_element_type=F32)
    for k in range(n_state_slab):
        z_ref[k] = z[:, k * LANES:(k + 1) * LANES]

    ar = al_ref[0:1, :]
    ai = al_ref[1:2, :]
    half = n_state_slab // 2

    def step(m, carry):
        xr, xi = carry
        rows = pl.ds(m, nb, stride=mc)
        for k in range(half):
            xp_ref[k, rows, :] = xr[:, k * LANES:(k + 1) * LANES]
            xp_ref[half + k, rows, :] = xi[:, k * LANES:(k + 1) * LANES]
        zr = jnp.concatenate([z_ref[k, rows, :] for k in range(half)], axis=1)
        zi = jnp.concatenate([z_ref[half + k, rows, :] for k in range(half)], axis=1)
        return ar * xr - ai * xi + zr, ar * xi + ai * xr + zi

    xr, xi = lax.fori_loop(0, mc, step, (st_ref[:, :SLAB_STATE], st_ref[:, SLAB_STATE:]))
    st_ref[:, :SLAB_STATE] = xr
    st_ref[:, SLAB_STATE:] = xi

    for k in range(n_state_slab):
        lhs_ref[:, CHUNK_K + k * LANES:CHUNK_K + (k + 1) * LANES] = xp_ref[k].astype(BF16)

    y = jnp.dot(lhs_ref[...], tv_ref[...], preferred_element_type=F32)
    for b in range(nb):
        for t in range(CHUNK):
            y_ref[b, pl.ds(t, mc, stride=CHUNK), :] = y[b * mc:(b + 1) * mc, t * LANES:(t + 1) * LANES]


def _s5_scan(u4, wmat, tv, al, nb, seq, tb):
    mc = tb // CHUNK
    rows = nb * mc
    n_state = 2 * SLAB_STATE
    kern = functools.partial(_s5_kernel, nb=nb, mc=mc)
    blk = lambda s, j: (s, 0, j, 0)
    per_slab = lambda s, j: (s, 0, 0)
    return pl.pallas_call(
        kern,
        grid=(N_SLAB, seq // tb),
        in_specs=[
            pl.BlockSpec((None, nb, tb, LANES), blk),
            pl.BlockSpec((None, CHUNK_K, n_state), per_slab),
            pl.BlockSpec((None, CHUNK_K + n_state, CHUNK_K), per_slab),
            pl.BlockSpec((None, 2, SLAB_STATE), per_slab),
        ],
        out_specs=pl.BlockSpec((None, nb, tb, LANES), blk),
        out_shape=jax.ShapeDtypeStruct((N_SLAB, nb, seq, LANES), F32),
        scratch_shapes=[
            pltpu.VMEM((rows, CHUNK_K + n_state), BF16),
            pltpu.VMEM((n_state // LANES, rows, LANES), F32),
            pltpu.VMEM((n_state // LANES, rows, LANES), F32),
            pltpu.VMEM((nb, n_state), F32),
        ],
        compiler_params=pltpu.CompilerParams(
            dimension_semantics=("arbitrary", "arbitrary"), vmem_limit_bytes=VMEM_LIMIT),
        name="s5_scan",
    )(u4, wmat, tv, al)


CONV_ROWS = 32


def _mix_kernel(x1_ref, a_ref, halo_ref, y4_ref, cw_ref, cb_ref, lng_ref, lnb_ref, cog_ref,
                gluw_ref, glub_ref, sog_ref, wout_ref, x2_ref, aext_ref, conv_ref, *, ts):
    first = pl.program_id(1) == 0
    aext_ref[0:HALO, :] = jnp.where(first, 0.0, halo_ref[...])
    aext_ref[HALO:, :] = a_ref[...]

    base = HALO - (CONV_WIDTH - 1)

    for r0 in range(0, ts, CONV_ROWS):
        acc = jnp.broadcast_to(cb_ref[...], (CONV_ROWS, D_CONV))
        for k in range(CONV_WIDTH):
            lo = r0 + base + k
            acc = acc + cw_ref[k:k + 1, :] * aext_ref[lo:lo + CONV_ROWS, :]
        conv_ref[r0:r0 + CONV_ROWS, :] = acc

    c = conv_ref[...]
    mu = jnp.mean(c, axis=-1, keepdims=True)
    xc = c - mu
    var = jnp.mean(xc * xc, axis=-1, keepdims=True)
    c = xc * lax.rsqrt(var + EPS) * lng_ref[...] + lnb_ref[...]
    c = c * jax.nn.sigmoid(c)
    a_out = _rmsnorm(c, cog_ref[...])

    y = jnp.concatenate([y4_ref[s] for s in range(N_SLAB)], axis=1)
    yg = jax.nn.gelu(y)
    gate = jnp.dot(yg.astype(BF16), gluw_ref[...], preferred_element_type=F32) + glub_ref[...]
    s_out = _rmsnorm(yg * jax.nn.sigmoid(gate), sog_ref[...])

    mixed = jnp.concatenate([a_out, s_out], axis=1).astype(BF16)
    x2_ref[...] = x1_ref[...] + jnp.dot(mixed, wout_ref[...], preferred_element_type=F32)


def _mix_out(x1, a, y4, cw, cb, lng, lnb, cog, gluw, glub, sog, wout, nb, seq, ts):
    nt = seq // ts
    hb = ts // HALO
    row = lambda b, j: (b * nt + j, 0)
    halo = lambda b, j: (jnp.maximum((b * nt + j) * hb - 1, 0), 0)
    kern = functools.partial(_mix_kernel, ts=ts)
    return pl.pallas_call(
        kern,
        grid=(nb, nt),
        in_specs=[
            pl.BlockSpec((ts, D_MODEL), row),
            pl.BlockSpec((ts, D_CONV), row),
            pl.BlockSpec((HALO, D_CONV), halo),
            pl.BlockSpec((N_SLAB, ts, LANES), lambda b, j: (0, b * nt + j, 0)),
            _const_spec((CONV_WIDTH, D_CONV)),
            _const_spec((1, D_CONV)),
            _const_spec((1, D_CONV)),
            _const_spec((1, D_CONV)),
            _const_spec((1, D_CONV)),
            _const_spec((D_SSM, D_SSM)),
            _const_spec((1, D_SSM)),
            _const_spec((1, D_SSM)),
            _const_spec((D_MODEL, D_MODEL)),
        ],
        out_specs=pl.BlockSpec((ts, D_MODEL), row),
        out_shape=jax.ShapeDtypeStruct((nb * seq, D_MODEL), F32),
        scratch_shapes=[
            pltpu.VMEM((HALO + ts, D_CONV), F32),
            pltpu.VMEM((ts, D_CONV), F32),
        ],
        compiler_params=pltpu.CompilerParams(
            dimension_semantics=("parallel", "parallel"), vmem_limit_bytes=VMEM_LIMIT),
        name="mix_out",
    )(x1, a, a, y4, cw, cb, lng, lnb, cog, gluw, glub, sog, wout)


def _ffn2_final_kernel(x_ref, g_ref, w1_ref, w3_ref, w2_ref, gf_ref, o_ref):
    x3 = _swiglu_residual(x_ref[...], g_ref[...], w1_ref, w3_ref, w2_ref)
    o_ref[...] = _rmsnorm(x3, gf_ref[...])


def _ffn2_final(x2d, g, w1, w3, w2, gf, tm):
    t = x2d.shape[0]
    row = lambda i: (i, 0)
    return pl.pallas_call(
        _ffn2_final_kernel,
        grid=(t // tm,),
        in_specs=[
            pl.BlockSpec((tm, D_MODEL), row),
            _const_spec((1, D_MODEL)),
            _const_spec((D_MODEL, D_FF)),
            _const_spec((D_MODEL, D_FF)),
            _const_spec((D_FF, D_MODEL)),
            _const_spec((1, D_MODEL)),
        ],
        out_specs=pl.BlockSpec((tm, D_MODEL), row),
        out_shape=jax.ShapeDtypeStruct((t, D_MODEL), F32),
        compiler_params=pltpu.CompilerParams(
            dimension_semantics=("parallel",), vmem_limit_bytes=VMEM_LIMIT),
        name="ffn2_final",
    )(x2d, g, w1, w3, w2, gf)


def kernel(x, norm_ffn1, ffn1_w1, ffn1_w3, ffn1_w2, norm_mix, w_in, conv_w, conv_b, conv_ln_g,
           conv_ln_b, conv_out_g, ssm_A_re, ssm_A_im, ssm_log_dt, ssm_B_re, ssm_B_im, ssm_C_re,
           ssm_C_im, ssm_D, ssm_glu_w, ssm_glu_b, ssm_out_g, w_out, norm_ffn2, ffn2_w1, ffn2_w3,
           ffn2_w2, norm_final):
    nb, seq, _ = x.shape
    depth = norm_ffn1.shape[0]
    assert depth == 1, "the final norm is fused into the last FFN; only depth 1 is laid out"
    tm = 256
    tb = 256
    ts = 256
    assert seq % tb == 0 and seq % ts == 0 and (nb * seq) % tm == 0
    assert tb % (CHUNK * 16) == 0 and ts % HALO == 0 and ts % CONV_ROWS == 0

    row = lambda v: v.reshape(1, -1).astype(F32)
    x2d = x.reshape(nb * seq, D_MODEL)
    for l in range(depth):
        x1, a, u4 = _ffn1_proj(
            x2d, row(norm_ffn1[l]), ffn1_w1[l].astype(BF16), ffn1_w3[l].astype(BF16),
            ffn1_w2[l].astype(BF16), row(norm_mix[l]), w_in[l].astype(BF16), tm)
        wmat, tv, al = _s5_weights(
            ssm_A_re[l], ssm_A_im[l], ssm_log_dt[l], ssm_B_re[l], ssm_B_im[l],
            ssm_C_re[l], ssm_C_im[l], ssm_D[l])
        y4 = _s5_scan(u4.reshape(N_SLAB, nb, seq, LANES), wmat, tv, al, nb, seq, tb)
        x2 = _mix_out(
            x1, a, y4.reshape(N_SLAB, nb * seq, LANES), conv_w[l], row(conv_b[l]),
            row(conv_ln_g[l]), row(conv_ln_b[l]), row(conv_out_g[l]), ssm_glu_w[l].astype(BF16),
            row(ssm_glu_b[l]), row(ssm_out_g[l]), w_out[l].astype(BF16), nb, seq, ts)
        x2d = _ffn2_final(
            x2, row(norm_ffn2[l]), ffn2_w1[l].astype(BF16), ffn2_w3[l].astype(BF16),
            ffn2_w2[l].astype(BF16), row(norm_final), tm)
    return x2d.reshape(nb, seq, D_MODEL)
```

```python
import functools

import jax
import jax.numpy as jnp
from jax import lax
from jax.experimental import pallas as pl
from jax.experimental.pallas import tpu as pltpu

D_MODEL = 1024
D_CONV = 512
D_SSM = 512
CONV_WIDTH = 31
SSM_GROUP = 16
SSM_GROUPS = 32
SSM_STATE = 64
D_FF = 2816
D_IN = 2 * D_CONV + D_SSM
FFN_RES = 0.5
EPS = 1e-6

LANES = 128
MXU_TILE = 256
SLAB_GROUPS = LANES // SSM_GROUP
N_SLAB = D_SSM // LANES
C_SLAB = D_CONV // LANES
SLAB_STATE = SLAB_GROUPS * SSM_STATE
CHUNK = 8
CHUNK_K = CHUNK * LANES
HALO = 32

VMEM_LIMIT = 56 * 1024 * 1024

F32 = jnp.float32
BF16 = jnp.bfloat16


def _rmsnorm(x, g):
    ms = jnp.mean(x * x, axis=-1, keepdims=True)
    return x * lax.rsqrt(ms + EPS) * g


def _swiglu_residual(x, g, w1_ref, w3_ref, w2_ref):
    h = _rmsnorm(x, g).astype(BF16)
    a = jnp.dot(h, w1_ref[...], preferred_element_type=F32)
    b = jnp.dot(h, w3_ref[...], preferred_element_type=F32)
    z = (a * jax.nn.sigmoid(a) * b).astype(BF16)
    o = jnp.dot(z, w2_ref[...], preferred_element_type=F32)
    return x + FFN_RES * o


def _const_spec(shape):
    nd = len(shape)
    return pl.BlockSpec(shape, lambda *_: (0,) * nd, pipeline_mode=pl.Buffered(1))


def _ffn1_proj_kernel(x_ref, g1_ref, w1_ref, w3_ref, w2_ref, gm_ref, win_ref,
                      x1_ref, a_ref, u_ref):
    x1 = _swiglu_residual(x_ref[...], g1_ref[...], w1_ref, w3_ref, w2_ref)
    x1_ref[...] = x1
    h2 = _rmsnorm(x1, gm_ref[...]).astype(BF16)
    proj = jnp.dot(h2, win_ref[...], preferred_element_type=F32)
    for s in range(C_SLAB):
        lo = s * LANES
        a_val = proj[:, lo:lo + LANES]
        a_gate = proj[:, D_CONV + lo:D_CONV + lo + LANES]
        a_ref[s] = a_val * jax.nn.sigmoid(a_gate)
    for s in range(N_SLAB):
        lo = 2 * D_CONV + s * LANES
        u_ref[s] = proj[:, lo:lo + LANES]


def _ffn1_proj(x2d, g1, w1, w3, w2, gm, win, tm):
    t = x2d.shape[0]
    row = lambda i: (i, 0)
    return pl.pallas_call(
        _ffn1_proj_kernel,
        grid=(t // tm,),
        in_specs=[
            pl.BlockSpec((tm, D_MODEL), row),
            _const_spec((1, D_MODEL)),
            _const_spec((D_MODEL, D_FF)),
            _const_spec((D_MODEL, D_FF)),
            _const_spec((D_FF, D_MODEL)),
            _const_spec((1, D_MODEL)),
            _const_spec((D_MODEL, D_IN)),
        ],
        out_specs=[
            pl.BlockSpec((tm, D_MODEL), row),
            pl.BlockSpec((C_SLAB, tm, LANES), lambda i: (0, i, 0)),
            pl.BlockSpec((N_SLAB, tm, LANES), lambda i: (0, i, 0)),
        ],
        out_shape=[
            jax.ShapeDtypeStruct((t, D_MODEL), F32),
            jax.ShapeDtypeStruct((C_SLAB, t, LANES), F32),
            jax.ShapeDtypeStruct((N_SLAB, t, LANES), F32),
        ],
        compiler_params=pltpu.CompilerParams(
            dimension_semantics=("parallel",), vmem_limit_bytes=VMEM_LIMIT),
        name="ffn1_proj",
    )(x2d, g1, w1, w3, w2, gm, win)


def _s5_compact(a_re, a_im, log_dt, b_re, b_im, c_re, c_im):
    L = CHUNK
    dt = jnp.exp(log_dt)[:, None]
    zr, zi = a_re * dt, a_im * dt
    mag = jnp.exp(zr)
    abar_r, abar_i = mag * jnp.cos(zi), mag * jnp.sin(zi)
    den = a_re * a_re + a_im * a_im
    nr = abar_r - 1.0
    coef_r = (nr * a_re + abar_i * a_im) / den
    coef_i = (abar_i * a_re - nr * a_im) / den
    bb_r = coef_r[..., None] * b_re - coef_i[..., None] * b_im
    bb_i = coef_r[..., None] * b_im + coef_i[..., None] * b_re
    n = jnp.arange(L + 1, dtype=F32)[:, None, None]
    pmag = jnp.exp(n * zr)
    pw_r, pw_i = pmag * jnp.cos(n * zi), pmag * jnp.sin(n * zi)
    cp_r = c_re[None] * pw_r[:, :, None, :] - c_im[None] * pw_i[:, :, None, :]
    cp_i = c_re[None] * pw_i[:, :, None, :] + c_im[None] * pw_r[:, :, None, :]
    vc = jnp.concatenate([cp_r, -cp_i], axis=-1)
    nrev = (L - 1) - n[:L]
    rmag = jnp.exp(nrev * zr)
    rv_r = (rmag * jnp.cos(nrev * zi))[:, :, None, :]
    rv_i = (rmag * jnp.sin(nrev * zi))[:, :, None, :]
    bt_r, bt_i = bb_r.transpose(0, 2, 1)[None], bb_i.transpose(0, 2, 1)[None]
    wc = jnp.concatenate([rv_r * bt_r - rv_i * bt_i, rv_r * bt_i + rv_i * bt_r], axis=-1)

    def per_slab(v):
        v = v.reshape(v.shape[0], N_SLAB, LANES, 2 * SSM_STATE)
        return v.transpose(1, 0, 2, 3).reshape(N_SLAB, -1, 2 * SSM_STATE)

    al = jnp.stack([pw_r[L].reshape(N_SLAB, SLAB_STATE), pw_i[L].reshape(N_SLAB, SLAB_STATE)], axis=1)
    return per_slab(wc), per_slab(vc), al


def _s5_embed_kernel(wc_ref, vc_ref, d_ref, w_ref, tv_ref):
    L = CHUNK

    def embed(c):
        rows = c.shape[0]
        row_group = (lax.broadcasted_iota(jnp.int32, (rows, LANES), 0) // SSM_GROUP) % SLAB_GROUPS
        lane_half = lax.broadcasted_iota(jnp.int32, (rows, LANES), 1) // SSM_STATE
        pieces = []
        for part in range(2):
            x = c[:, part * SSM_STATE:(part + 1) * SSM_STATE]
            xx = jnp.concatenate([x] * (LANES // SSM_STATE), axis=1)
            for k in range(SLAB_STATE // LANES):
                own = row_group == (LANES // SSM_STATE) * k + lane_half
                pieces.append(jnp.where(own, xx, 0.0))
        return jnp.concatenate(pieces, axis=1)

    w = embed(wc_ref[...])
    vt = embed(vc_ref[...])
    w_ref[...] = w.astype(BF16)

    w0 = w[(L - 1) * LANES:, :]
    nt = (((1,), (1,)), ((), ()))
    ri = lax.broadcasted_iota(jnp.int32, (LANES, LANES), 0)
    ci = lax.broadcasted_iota(jnp.int32, (LANES, LANES), 1)
    zero = jnp.zeros((LANES, LANES), BF16)
    for n in range(L):
        k_n = lax.dot_general(w0, vt[n * LANES:(n + 1) * LANES, :], nt,
                              precision=lax.Precision.HIGHEST, preferred_element_type=F32)
        if n == 0:
            k_n = k_n + jnp.where(ri == ci, d_ref[...], 0.0)
        k_n = k_n.astype(BF16)
        for s in range(L - n):
            t = s + n
            tv_ref[s * LANES:(s + 1) * LANES, t * LANES:(t + 1) * LANES] = k_n
    for s in range(1, L):
        for t in range(s):
            tv_ref[s * LANES:(s + 1) * LANES, t * LANES:(t + 1) * LANES] = zero
    tv_ref[L * LANES:, :] = vt[LANES:, :].T.astype(BF16)


def _s5_embed(wc, vc, d4):
    n_state = 2 * SLAB_STATE
    per_slab = lambda s: (s, 0, 0)
    return pl.pallas_call(
        _s5_embed_kernel,
        grid=(N_SLAB,),
        in_specs=[
            pl.BlockSpec((None, CHUNK_K, 2 * SSM_STATE), per_slab),
            pl.BlockSpec((None, CHUNK_K + LANES, 2 * SSM_STATE), per_slab),
            pl.BlockSpec((None, 1, LANES), per_slab),
        ],
        out_specs=[
            pl.BlockSpec((None, CHUNK_K, n_state), per_slab),
            pl.BlockSpec((None, CHUNK_K + n_state, CHUNK_K), per_slab),
        ],
        out_shape=[
            jax.ShapeDtypeStruct((N_SLAB, CHUNK_K, n_state), BF16),
            jax.ShapeDtypeStruct((N_SLAB, CHUNK_K + n_state, CHUNK_K), BF16),
        ],
        compiler_params=pltpu.CompilerParams(
            dimension_semantics=("parallel",), vmem_limit_bytes=VMEM_LIMIT),
        name="s5_embed",
    )(wc, vc, d4)


def _s5_kernel(u_ref, w_ref, tv_ref, al_ref, y_ref, lhs_ref, xpb_ref, z_ref, xp_ref, st_ref,
               *, nb, mc):
    n_state_slab = 2 * SLAB_STATE // LANES
    half = n_state_slab // 2
    pitch = mc + 8

    @pl.when(pl.program_id(1) == 0)
    def _():
        st_ref[...] = jnp.zeros_like(st_ref)

    for b in range(nb):
        for s in range(CHUNK):
            piece = u_ref[b, pl.ds(s, mc, stride=CHUNK), :]
            lhs_ref[b * mc:(b + 1) * mc, s * LANES:(s + 1) * LANES] = piece.astype(BF16)

    z = jnp.dot(lhs_ref[...], w_ref[...], preferred_element_type=F32)
    for b in range(nb):
        for k in range(n_state_slab):
            z_ref[k, b * pitch:b * pitch + mc, :] = z[b * mc:(b + 1) * mc, k * LANES:(k + 1) * LANES]

    y_cols = []
    for n in range(CHUNK_K // MXU_TILE):
        kk = (n + 1) * MXU_TILE
        y_cols.append(jnp.dot(lhs_ref[:, :kk], tv_ref[:kk, n * MXU_TILE:(n + 1) * MXU_TILE],
                              preferred_element_type=F32))

    ar = al_ref[0:1, :]
    ai = al_ref[1:2, :]
    xr = st_ref[:, :SLAB_STATE]
    xi = st_ref[:, SLAB_STATE:]
    for m in range(mc):
        rows = pl.ds(m, nb, stride=pitch)
        for k in range(half):
            xp_ref[k, rows, :] = xr[:, k * LANES:(k + 1) * LANES]
            xp_ref[half + k, rows, :] = xi[:, k * LANES:(k + 1) * LANES]
        zr = jnp.concatenate([z_ref[k, rows, :] for k in range(half)], axis=1)
        zi = jnp.concatenate([z_ref[half + k, rows, :] for k in range(half)], axis=1)
        xr, xi = ar * xr - ai * xi + zr, ar * xi + ai * xr + zi
    st_ref[:, :SLAB_STATE] = xr
    st_ref[:, SLAB_STATE:] = xi

    for b in range(nb):
        for k in range(n_state_slab):
            xpb_ref[b * mc:(b + 1) * mc, k * LANES:(k + 1) * LANES] = (
                xp_ref[k, b * pitch:b * pitch + mc, :].astype(BF16))

    y = (jnp.concatenate(y_cols, axis=1)
         + jnp.dot(xpb_ref[...], tv_ref[CHUNK_K:, :], preferred_element_type=F32))
    for b in range(nb):
        for t in range(CHUNK):
            y_ref[b, pl.ds(t, mc, stride=CHUNK), :] = y[b * mc:(b + 1) * mc, t * LANES:(t + 1) * LANES]


def _s5_scan(u4, wmat, tv, al, nb, seq, tb):
    mc = tb // CHUNK
    rows = nb * mc
    n_state = 2 * SLAB_STATE
    kern = functools.partial(_s5_kernel, nb=nb, mc=mc)
    blk = lambda s, j: (s, 0, j, 0)
    per_slab = lambda s, j: (s, 0, 0)
    return pl.pallas_call(
        kern,
        grid=(N_SLAB, seq // tb),
        in_specs=[
            pl.BlockSpec((None, nb, tb, LANES), blk),
            pl.BlockSpec((None, CHUNK_K, n_state), per_slab),
            pl.BlockSpec((None, CHUNK_K + n_state, CHUNK_K), per_slab),
            pl.BlockSpec((None, 2, SLAB_STATE), per_slab),
        ],
        out_specs=pl.BlockSpec((None, nb, tb, LANES), blk),
        out_shape=jax.ShapeDtypeStruct((N_SLAB, nb, seq, LANES), F32),
        scratch_shapes=[
            pltpu.VMEM((rows, CHUNK_K), BF16),
            pltpu.VMEM((rows, n_state), BF16),
            pltpu.VMEM((n_state // LANES, nb * (mc + 8), LANES), F32),
            pltpu.VMEM((n_state // LANES, nb * (mc + 8), LANES), F32),
            pltpu.VMEM((nb, n_state), F32),
        ],
        compiler_params=pltpu.CompilerParams(
            dimension_semantics=("arbitrary", "arbitrary"), vmem_limit_bytes=VMEM_LIMIT),
        name="s5_scan",
    )(u4, wmat, tv, al)


CONV_ROWS = 32
CONV_STRIDE = 2


def _mix_kernel(x1_ref, a_ref, halo_ref, y4_ref, cw_ref, cb_ref, lng_ref, lnb_ref, cog_ref,
                gluw_ref, glub_ref, sog_ref, wout_ref, x2_ref, aext_ref, conv_ref, *, ts):
    first = pl.program_id(1) == 0
    aext_ref[:, 0:HALO, :] = jnp.where(first, 0.0, halo_ref[...])
    aext_ref[:, HALO:, :] = a_ref[...]

    base = HALO - (CONV_WIDTH - 1)
    span = CONV_ROWS * CONV_STRIDE

    def conv_rows(j, _):
        r0 = j * span
        for s in range(C_SLAB):
            lanes = slice(s * LANES, (s + 1) * LANES)
            for ph in range(CONV_STRIDE):
                acc = jnp.broadcast_to(cb_ref[:, lanes], (CONV_ROWS, LANES))
                for k in range(CONV_WIDTH):
                    rows = pl.ds(r0 + ph + base + k, CONV_ROWS, stride=CONV_STRIDE)
                    acc = acc + cw_ref[k:k + 1, lanes] * aext_ref[s, rows, :]
                conv_ref[s, pl.ds(r0 + ph, CONV_ROWS, stride=CONV_STRIDE), :] = acc
        return 0

    lax.fori_loop(0, ts // span, conv_rows, 0)

    c = jnp.concatenate([conv_ref[s] for s in range(C_SLAB)], axis=1)
    mu = jnp.mean(c, axis=-1, keepdims=True)
    xc = c - mu
    var = jnp.mean(xc * xc, axis=-1, keepdims=True)
    c = xc * lax.rsqrt(var + EPS) * lng_ref[...] + lnb_ref[...]
    c = c * jax.nn.sigmoid(c)
    a_out = _rmsnorm(c, cog_ref[...])

    y = jnp.concatenate([y4_ref[s] for s in range(N_SLAB)], axis=1)
    yg = jax.nn.gelu(y)
    gate = jnp.dot(yg.astype(BF16), gluw_ref[...], preferred_element_type=F32) + glub_ref[...]
    s_out = _rmsnorm(yg * jax.nn.sigmoid(gate), sog_ref[...])

    mixed = jnp.concatenate([a_out, s_out], axis=1).astype(BF16)
    x2_ref[...] = x1_ref[...] + jnp.dot(mixed, wout_ref[...], preferred_element_type=F32)


def _mix_out(x1, a, y4, cw, cb, lng, lnb, cog, gluw, glub, sog, wout, nb, seq, ts):
    nt = seq // ts
    hb = ts // HALO
    row = lambda b, j: (b * nt + j, 0)
    slab_row = lambda b, j: (0, b * nt + j, 0)
    halo = lambda b, j: (0, jnp.maximum((b * nt + j) * hb - 1, 0), 0)
    kern = functools.partial(_mix_kernel, ts=ts)
    return pl.pallas_call(
        kern,
        grid=(nb, nt),
        in_specs=[
            pl.BlockSpec((ts, D_MODEL), row),
            pl.BlockSpec((C_SLAB, ts, LANES), slab_row),
            pl.BlockSpec((C_SLAB, HALO, LANES), halo),
            pl.BlockSpec((N_SLAB, ts, LANES), slab_row),
            _const_spec((CONV_WIDTH, D_CONV)),
            _const_spec((1, D_CONV)),
            _const_spec((1, D_CONV)),
            _const_spec((1, D_CONV)),
            _const_spec((1, D_CONV)),
            _const_spec((D_SSM, D_SSM)),
            _const_spec((1, D_SSM)),
            _const_spec((1, D_SSM)),
            _const_spec((D_MODEL, D_MODEL)),
        ],
        out_specs=pl.BlockSpec((ts, D_MODEL), row),
        out_shape=jax.ShapeDtypeStruct((nb * seq, D_MODEL), F32),
        scratch_shapes=[
            pltpu.VMEM((C_SLAB, HALO + ts, LANES), F32),
            pltpu.VMEM((C_SLAB, ts, LANES), F32),
        ],
        compiler_params=pltpu.CompilerParams(
            dimension_semantics=("parallel", "parallel"), vmem_limit_bytes=VMEM_LIMIT),
        name="mix_out",
    )(x1, a, a, y4, cw, cb, lng, lnb, cog, gluw, glub, sog, wout)


def _ffn2_final_kernel(x_ref, g_ref, w1_ref, w3_ref, w2_ref, gf_ref, o_ref):
    x3 = _swiglu_residual(x_ref[...], g_ref[...], w1_ref, w3_ref, w2_ref)
    o_ref[...] = _rmsnorm(x3, gf_ref[...])


def _ffn2_final(x2d, g, w1, w3, w2, gf, tm):
    t = x2d.shape[0]
    row = lambda i: (i, 0)
    return pl.pallas_call(
        _ffn2_final_kernel,
        grid=(t // tm,),
        in_specs=[
            pl.BlockSpec((tm, D_MODEL), row),
            _const_spec((1, D_MODEL)),
            _const_spec((D_MODEL, D_FF)),
            _const_spec((D_MODEL, D_FF)),
            _const_spec((D_FF, D_MODEL)),
            _const_spec((1, D_MODEL)),
        ],
        out_specs=pl.BlockSpec((tm, D_MODEL), row),
        out_shape=jax.ShapeDtypeStruct((t, D_MODEL), F32),
        compiler_params=pltpu.CompilerParams(
            dimension_semantics=("parallel",), vmem_limit_bytes=VMEM_LIMIT),
        name="ffn2_final",
    )(x2d, g, w1, w3, w2, gf)


def kernel(x, norm_ffn1, ffn1_w1, ffn1_w3, ffn1_w2, norm_mix, w_in, conv_w, conv_b, conv_ln_g,
           conv_ln_b, conv_out_g, ssm_A_re, ssm_A_im, ssm_log_dt, ssm_B_re, ssm_B_im, ssm_C_re,
           ssm_C_im, ssm_D, ssm_glu_w, ssm_glu_b, ssm_out_g, w_out, norm_ffn2, ffn2_w1, ffn2_w3,
           ffn2_w2, norm_final):
    nb, seq, _ = x.shape
    depth = norm_ffn1.shape[0]
    assert depth == 1, "the final norm is fused into the last FFN; only depth 1 is laid out"
    tm = 512
    tb = 256
    ts = 256
    assert seq % tb == 0 and seq % ts == 0 and (nb * seq) % tm == 0
    assert tb % (CHUNK * 16) == 0 and ts % HALO == 0 and ts % (CONV_ROWS * CONV_STRIDE) == 0

    row = lambda v: v.reshape(1, -1).astype(F32)
    x2d = x.reshape(nb * seq, D_MODEL)
    for l in range(depth):
        x1, a, u4 = _ffn1_proj(
            x2d, row(norm_ffn1[l]), ffn1_w1[l].astype(BF16), ffn1_w3[l].astype(BF16),
            ffn1_w2[l].astype(BF16), row(norm_mix[l]), w_in[l].astype(BF16), tm)
        wc, vc, al = _s5_compact(
            ssm_A_re[l], ssm_A_im[l], ssm_log_dt[l], ssm_B_re[l], ssm_B_im[l],
            ssm_C_re[l], ssm_C_im[l])
        wmat, tv = _s5_embed(wc, vc, ssm_D[l].reshape(N_SLAB, 1, LANES))
        y4 = _s5_scan(u4.reshape(N_SLAB, nb, seq, LANES), wmat, tv, al, nb, seq, tb)
        x2 = _mix_out(
            x1, a, y4.reshape(N_SLAB, nb * seq, LANES), conv_w[l], row(conv_b[l]),
            row(conv_ln_g[l]), row(conv_ln_b[l]), row(conv_out_g[l]), ssm_glu_w[l].astype(BF16),
            row(ssm_glu_b[l]), row(ssm_out_g[l]), w_out[l].astype(BF16), nb, seq, ts)
        x2d = _ffn2_final(
            x2, row(norm_ffn2[l]), ffn2_w1[l].astype(BF16), ffn2_w3[l].astype(BF16),
            ffn2_w2[l].astype(BF16), row(norm_final), tm)
    return x2d.reshape(nb, seq, D_MODEL)
```

```python
import functools

import jax
import jax.numpy as jnp
from jax import lax
from jax.experimental import pallas as pl
from jax.experimental.pallas import tpu as pltpu

D_MODEL = 1024
D_CONV = 512
D_SSM = 512
CONV_WIDTH = 31
SSM_GROUP = 16
SSM_GROUPS = 32
SSM_STATE = 64
D_FF = 2816
D_IN = 2 * D_CONV + D_SSM
FFN_RES = 0.5
EPS = 1e-6

LANES = 128
MXU_TILE = 256
SLAB_GROUPS = LANES // SSM_GROUP
N_SLAB = D_SSM // LANES
C_SLAB = D_CONV // LANES
SLAB_STATE = SLAB_GROUPS * SSM_STATE
CHUNK = 8
CHUNK_K = CHUNK * LANES
HALO = 32

VMEM_LIMIT = 56 * 1024 * 1024

F32 = jnp.float32
BF16 = jnp.bfloat16


def _rmsnorm(x, g):
    ms = jnp.mean(x * x, axis=-1, keepdims=True)
    return x * lax.rsqrt(ms + EPS) * g


def _swiglu_residual(x, g, w1_ref, w3_ref, w2_ref):
    h = _rmsnorm(x, g).astype(BF16)
    a = jnp.dot(h, w1_ref[...], preferred_element_type=F32)
    b = jnp.dot(h, w3_ref[...], preferred_element_type=F32)
    z = (a * jax.nn.sigmoid(a) * b).astype(BF16)
    o = jnp.dot(z, w2_ref[...], preferred_element_type=F32)
    return x + FFN_RES * o


def _const_spec(shape):
    nd = len(shape)
    return pl.BlockSpec(shape, lambda *_: (0,) * nd, pipeline_mode=pl.Buffered(1))


def _ffn1_proj_kernel(x_ref, g1_ref, w1_ref, w3_ref, w2_ref, gm_ref, win_ref,
                      x1_ref, a_ref, u_ref):
    x1 = _swiglu_residual(x_ref[...], g1_ref[...], w1_ref, w3_ref, w2_ref)
    x1_ref[...] = x1
    h2 = _rmsnorm(x1, gm_ref[...]).astype(BF16)
    proj = jnp.dot(h2, win_ref[...], preferred_element_type=F32)
    for s in range(C_SLAB):
        lo = s * LANES
        a_val = proj[:, lo:lo + LANES]
        a_gate = proj[:, D_CONV + lo:D_CONV + lo + LANES]
        a_ref[s] = a_val * jax.nn.sigmoid(a_gate)
    for s in range(N_SLAB):
        lo = 2 * D_CONV + s * LANES
        u_ref[s] = proj[:, lo:lo + LANES]


def _ffn1_proj(x2d, g1, w1, w3, w2, gm, win, tm):
    t = x2d.shape[0]
    row = lambda i: (i, 0)
    return pl.pallas_call(
        _ffn1_proj_kernel,
        grid=(t // tm,),
        in_specs=[
            pl.BlockSpec((tm, D_MODEL), row),
            _const_spec((1, D_MODEL)),
            _const_spec((D_MODEL, D_FF)),
            _const_spec((D_MODEL, D_FF)),
            _const_spec((D_FF, D_MODEL)),
            _const_spec((1, D_MODEL)),
            _const_spec((D_MODEL, D_IN)),
        ],
        out_specs=[
            pl.BlockSpec((tm, D_MODEL), row),
            pl.BlockSpec((C_SLAB, tm, LANES), lambda i: (0, i, 0)),
            pl.BlockSpec((N_SLAB, tm, LANES), lambda i: (0, i, 0)),
        ],
        out_shape=[
            jax.ShapeDtypeStruct((t, D_MODEL), F32),
            jax.ShapeDtypeStruct((C_SLAB, t, LANES), F32),
            jax.ShapeDtypeStruct((N_SLAB, t, LANES), F32),
        ],
        compiler_params=pltpu.CompilerParams(
            dimension_semantics=("parallel",), vmem_limit_bytes=VMEM_LIMIT),
        name="ffn1_proj",
    )(x2d, g1, w1, w3, w2, gm, win)


def _s5_compact(a_re, a_im, log_dt, b_re, b_im, c_re, c_im):
    L = CHUNK
    dt = jnp.exp(log_dt)[:, None]
    zr, zi = a_re * dt, a_im * dt
    mag = jnp.exp(zr)
    abar_r, abar_i = mag * jnp.cos(zi), mag * jnp.sin(zi)
    den = a_re * a_re + a_im * a_im
    nr = abar_r - 1.0
    coef_r = (nr * a_re + abar_i * a_im) / den
    coef_i = (abar_i * a_re - nr * a_im) / den
    bb_r = coef_r[..., None] * b_re - coef_i[..., None] * b_im
    bb_i = coef_r[..., None] * b_im + coef_i[..., None] * b_re
    n = jnp.arange(L + 1, dtype=F32)[:, None, None]
    pmag = jnp.exp(n * zr)
    pw_r, pw_i = pmag * jnp.cos(n * zi), pmag * jnp.sin(n * zi)
    cp_r = c_re[None] * pw_r[:, :, None, :] - c_im[None] * pw_i[:, :, None, :]
    cp_i = c_re[None] * pw_i[:, :, None, :] + c_im[None] * pw_r[:, :, None, :]
    vc = jnp.concatenate([cp_r, -cp_i], axis=-1)
    nrev = (L - 1) - n[:L]
    rmag = jnp.exp(nrev * zr)
    rv_r = (rmag * jnp.cos(nrev * zi))[:, :, None, :]
    rv_i = (rmag * jnp.sin(nrev * zi))[:, :, None, :]
    bt_r, bt_i = bb_r.transpose(0, 2, 1)[None], bb_i.transpose(0, 2, 1)[None]
    wc = jnp.concatenate([rv_r * bt_r - rv_i * bt_i, rv_r * bt_i + rv_i * bt_r], axis=-1)

    def per_slab(v):
        v = v.reshape(v.shape[0], N_SLAB, LANES, 2 * SSM_STATE)
        return v.transpose(1, 0, 2, 3).reshape(N_SLAB, -1, 2 * SSM_STATE)

    al = jnp.stack([pw_r[L].reshape(N_SLAB, SLAB_STATE), pw_i[L].reshape(N_SLAB, SLAB_STATE)], axis=1)
    return per_slab(wc), per_slab(vc), al


def _s5_embed_kernel(wc_ref, vc_ref, d_ref, w_ref, tv_ref):
    L = CHUNK

    def embed(c):
        rows = c.shape[0]
        row_group = (lax.broadcasted_iota(jnp.int32, (rows, LANES), 0) // SSM_GROUP) % SLAB_GROUPS
        lane_half = lax.broadcasted_iota(jnp.int32, (rows, LANES), 1) // SSM_STATE
        pieces = []
        for part in range(2):
            x = c[:, part * SSM_STATE:(part + 1) * SSM_STATE]
            xx = jnp.concatenate([x] * (LANES // SSM_STATE), axis=1)
            for k in range(SLAB_STATE // LANES):
                own = row_group == (LANES // SSM_STATE) * k + lane_half
                pieces.append(jnp.where(own, xx, 0.0))
        return jnp.concatenate(pieces, axis=1)

    w = embed(wc_ref[...])
    vt = embed(vc_ref[...])
    w_ref[...] = w.astype(BF16)

    w0 = w[(L - 1) * LANES:, :]
    nt = (((1,), (1,)), ((), ()))
    ri = lax.broadcasted_iota(jnp.int32, (LANES, LANES), 0)
    ci = lax.broadcasted_iota(jnp.int32, (LANES, LANES), 1)
    zero = jnp.zeros((LANES, LANES), BF16)
    for n in range(L):
        k_n = lax.dot_general(w0, vt[n * LANES:(n + 1) * LANES, :], nt,
                              precision=lax.Precision.HIGHEST, preferred_element_type=F32)
        if n == 0:
            k_n = k_n + jnp.where(ri == ci, d_ref[...], 0.0)
        k_n = k_n.astype(BF16)
        for s in range(L - n):
            t = s + n
            tv_ref[s * LANES:(s + 1) * LANES, t * LANES:(t + 1) * LANES] = k_n
    for s in range(1, L):
        for t in range(s):
            tv_ref[s * LANES:(s + 1) * LANES, t * LANES:(t + 1) * LANES] = zero
    tv_ref[L * LANES:, :] = vt[LANES:, :].T.astype(BF16)


def _s5_embed(wc, vc, d4):
    n_state = 2 * SLAB_STATE
    per_slab = lambda s: (s, 0, 0)
    return pl.pallas_call(
        _s5_embed_kernel,
        grid=(N_SLAB,),
        in_specs=[
            pl.BlockSpec((None, CHUNK_K, 2 * SSM_STATE), per_slab),
            pl.BlockSpec((None, CHUNK_K + LANES, 2 * SSM_STATE), per_slab),
            pl.BlockSpec((None, 1, LANES), per_slab),
        ],
        out_specs=[
            pl.BlockSpec((None, CHUNK_K, n_state), per_slab),
            pl.BlockSpec((None, CHUNK_K + n_state, CHUNK_K), per_slab),
        ],
        out_shape=[
            jax.ShapeDtypeStruct((N_SLAB, CHUNK_K, n_state), BF16),
            jax.ShapeDtypeStruct((N_SLAB, CHUNK_K + n_state, CHUNK_K), BF16),
        ],
        compiler_params=pltpu.CompilerParams(
            dimension_semantics=("parallel",), vmem_limit_bytes=VMEM_LIMIT),
        name="s5_embed",
    )(wc, vc, d4)


def _s5_kernel(u_ref, w_ref, tv_ref, al_ref, y_ref, lhs_ref, xpb_ref, z_ref, xp_ref, st_ref,
               *, nb, mc):
    n_state_slab = 2 * SLAB_STATE // LANES
    half = n_state_slab // 2
    pitch = mc + 8

    @pl.when(pl.program_id(1) == 0)
    def _():
        st_ref[...] = jnp.zeros_like(st_ref)

    for b in range(nb):
        for s in range(CHUNK):
            piece = u_ref[b, pl.ds(s, mc, stride=CHUNK), :]
            lhs_ref[b * mc:(b + 1) * mc, s * LANES:(s + 1) * LANES] = piece.astype(BF16)

    z = jnp.dot(lhs_ref[...], w_ref[...], preferred_element_type=F32)
    for b in range(nb):
        for k in range(n_state_slab):
            z_ref[k, b * pitch:b * pitch + mc, :] = z[b * mc:(b + 1) * mc, k * LANES:(k + 1) * LANES]

    y_cols = []
    for n in range(CHUNK_K // MXU_TILE):
        kk = (n + 1) * MXU_TILE
        y_cols.append(jnp.dot(lhs_ref[:, :kk], tv_ref[:kk, n * MXU_TILE:(n + 1) * MXU_TILE],
                              preferred_element_type=F32))

    ar = al_ref[0:1, :]
    ai = al_ref[1:2, :]
    xr = st_ref[:, :SLAB_STATE]
    xi = st_ref[:, SLAB_STATE:]
    for m in range(mc):
        rows = pl.ds(m, nb, stride=pitch)
        for k in range(half):
            xp_ref[k, rows, :] = xr[:, k * LANES:(k + 1) * LANES]
            xp_ref[half + k, rows, :] = xi[:, k * LANES:(k + 1) * LANES]
        zr = jnp.concatenate([z_ref[k, rows, :] for k in range(half)], axis=1)
        zi = jnp.concatenate([z_ref[half + k, rows, :] for k in range(half)], axis=1)
        xr, xi = ar * xr - ai * xi + zr, ar * xi + ai * xr + zi
    st_ref[:, :SLAB_STATE] = xr
    st_ref[:, SLAB_STATE:] = xi

    for b in range(nb):
        for k in range(n_state_slab):
            xpb_ref[b * mc:(b + 1) * mc, k * LANES:(k + 1) * LANES] = (
                xp_ref[k, b * pitch:b * pitch + mc, :].astype(BF16))

    y = (jnp.concatenate(y_cols, axis=1)
         + jnp.dot(xpb_ref[...], tv_ref[CHUNK_K:, :], preferred_element_type=F32))
    for b in range(nb):
        for t in range(CHUNK):
            y_ref[b, pl.ds(t, mc, stride=CHUNK), :] = y[b * mc:(b + 1) * mc, t * LANES:(t + 1) * LANES]


def _s5_scan(u4, wmat, tv, al, nb, seq, tb):
    mc = tb // CHUNK
    rows = nb * mc
    n_state = 2 * SLAB_STATE
    kern = functools.partial(_s5_kernel, nb=nb, mc=mc)
    blk = lambda s, j: (s, 0, j, 0)
    per_slab = lambda s, j: (s, 0, 0)
    return pl.pallas_call(
        kern,
        grid=(N_SLAB, seq // tb),
        in_specs=[
            pl.BlockSpec((None, nb, tb, LANES), blk),
            pl.BlockSpec((None, CHUNK_K, n_state), per_slab),
            pl.BlockSpec((None, CHUNK_K + n_state, CHUNK_K), per_slab),
            pl.BlockSpec((None, 2, SLAB_STATE), per_slab),
        ],
        out_specs=pl.BlockSpec((None, nb, tb, LANES), blk),
        out_shape=jax.ShapeDtypeStruct((N_SLAB, nb, seq, LANES), F32),
        scratch_shapes=[
            pltpu.VMEM((rows, CHUNK_K), BF16),
            pltpu.VMEM((rows, n_state), BF16),
            pltpu.VMEM((n_state // LANES, nb * (mc + 8), LANES), F32),
            pltpu.VMEM((n_state // LANES, nb * (mc + 8), LANES), F32),
            pltpu.VMEM((nb, n_state), F32),
        ],
        compiler_params=pltpu.CompilerParams(
            dimension_semantics=("arbitrary", "arbitrary"), vmem_limit_bytes=VMEM_LIMIT),
        name="s5_scan",
    )(u4, wmat, tv, al)


CONV_ROWS = 16
CONV_STRIDE = 2
MIX_ROWS = CONV_ROWS * CONV_STRIDE


def _order_token(v):
    t = v[:, :LANES]
    for lo in range(LANES, v.shape[1], LANES):
        t = jnp.maximum(t, v[:, lo:lo + LANES])
    out = t[:CONV_ROWS]
    for lo in range(CONV_ROWS, t.shape[0], CONV_ROWS):
        out = jnp.maximum(out, t[lo:lo + CONV_ROWS])
    return out


def _conv_chain(r0, s, ph, after, cw_ref, cb_ref, aext_ref, conv_ref):
    base = HALO - (CONV_WIDTH - 1)
    lanes = slice(s * LANES, (s + 1) * LANES)
    acc = jnp.broadcast_to(cb_ref[:, lanes], (CONV_ROWS, LANES))
    for dep in after:
        bits = pltpu.bitcast(dep, jnp.uint32)
        bits = lax.shift_right_logical(lax.shift_right_logical(bits, jnp.uint32(16)), jnp.uint32(16))
        acc = pltpu.bitcast(pltpu.bitcast(acc, jnp.uint32) + bits, F32)
    for k in range(CONV_WIDTH):
        rows = pl.ds(r0 + ph + base + k, CONV_ROWS, stride=CONV_STRIDE)
        acc = acc + cw_ref[k:k + 1, lanes] * aext_ref[s, rows, :]
    conv_ref[s, pl.ds(r0 + ph, CONV_ROWS, stride=CONV_STRIDE), :] = acc
    return acc


def _mix_rows(r0, conv_ref, y4_ref, lng_ref, lnb_ref, cog_ref, gluw_ref, glub_ref, sog_ref, dst_ref):
    rows = slice(r0, r0 + MIX_ROWS)
    c = jnp.concatenate([conv_ref[s, rows, :] for s in range(C_SLAB)], axis=1)
    mu = jnp.mean(c, axis=-1, keepdims=True)
    xc = c - mu
    var = jnp.mean(xc * xc, axis=-1, keepdims=True)
    c = xc * lax.rsqrt(var + EPS) * lng_ref[...] + lnb_ref[...]
    c = c * jax.nn.sigmoid(c)
    a_out = _rmsnorm(c, cog_ref[...])

    y = jnp.concatenate([y4_ref[s, rows, :] for s in range(N_SLAB)], axis=1)
    yg = jax.nn.gelu(y)
    gate = jnp.dot(yg.astype(BF16), gluw_ref[...], preferred_element_type=F32) + glub_ref[...]
    s_out = _rmsnorm(yg * jax.nn.sigmoid(gate), sog_ref[...])
    dst_ref[rows, :] = jnp.concatenate([a_out, s_out], axis=1).astype(BF16)


def _mix_ffn2_kernel(x1_ref, a_ref, halo_ref, y4_ref, cw_ref, cb_ref, lng_ref, lnb_ref, cog_ref,
                     gluw_ref, glub_ref, sog_ref, wout_ref, g2_ref, w1_ref, w3_ref, w2_ref, gf_ref,
                     o_ref, aext_ref, conv_ref, mixed_ref, z_ref, *, ts, nt, n_tiles):
    g = pl.program_id(0)

    @pl.when(g == 0)
    def _():
        mixed_ref[...] = jnp.zeros_like(mixed_ref)

    first = jnp.minimum(g, n_tiles - 1) % nt == 0
    aext_ref[:, 0:HALO, :] = jnp.where(first, 0.0, halo_ref[...])
    aext_ref[:, HALO:, :] = a_ref[...]

    after = []

    def conv_piece(r0, s, ph):
        acc = _conv_chain(r0, s, ph, after, cw_ref, cb_ref, aext_ref, conv_ref)
        after[:] = [acc]

    mix_pieces = []
    for r0 in range(0, ts, MIX_ROWS):
        for s in range(C_SLAB):
            for ph in range(CONV_STRIDE):
                mix_pieces.append(functools.partial(conv_piece, r0, s, ph))
        mix_pieces.append(functools.partial(
            _mix_rows, r0, conv_ref, y4_ref, lng_ref, lnb_ref, cog_ref, gluw_ref, glub_ref,
            sog_ref, mixed_ref.at[g % 2]))
    n_out = D_MODEL // MXU_TILE
    n_ff = D_FF // MXU_TILE
    n_slots = n_out + n_ff
    done = [0]

    def fill(slot, result):
        upto = min(len(mix_pieces), (len(mix_pieces) * (slot + 1)) // n_slots)
        if upto == done[0]:
            return
        after.append(_order_token(result))
        for piece in mix_pieces[done[0]:upto]:
            piece()
        done[0] = upto

    col = lambda n: slice(n * MXU_TILE, (n + 1) * MXU_TILE)
    mixed_prev = mixed_ref[(g + 1) % 2]
    x2_cols = []
    for n in range(n_out):
        x2_cols.append(x1_ref[:, col(n)]
                       + jnp.dot(mixed_prev, wout_ref[:, col(n)], preferred_element_type=F32))
        fill(n, x2_cols[n])
    x2 = jnp.concatenate(x2_cols, axis=1)
    h = _rmsnorm(x2, g2_ref[...]).astype(BF16)
    for n in range(n_ff):
        a = jnp.dot(h, w1_ref[:, col(n)], preferred_element_type=F32)
        b = jnp.dot(h, w3_ref[:, col(n)], preferred_element_type=F32)
        zf = a * jax.nn.sigmoid(a) * b
        z_ref[:, col(n)] = zf.astype(BF16)
        fill(n_out + n, zf)
    z = z_ref[...]
    x3_cols = []
    for n in range(n_out):
        o = jnp.dot(z, w2_ref[:, col(n)], preferred_element_type=F32)
        x3_cols.append(x2_cols[n] + FFN_RES * o)
        if n < n_out - 1:
            fill(n_out + n_ff + n, x3_cols[n])
    o_ref[...] = _rmsnorm(jnp.concatenate(x3_cols, axis=1), gf_ref[...])


def _mix_ffn2(x1, a, y4, cw, cb, lng, lnb, cog, gluw, glub, sog, wout, g2, w1, w3, w2, gf,
              nb, seq, ts):
    nt = seq // ts
    n_tiles = nb * nt
    hb = ts // HALO
    cur = lambda g: jnp.minimum(g, n_tiles - 1)
    prev = lambda g: jnp.maximum(g - 1, 0)
    kern = functools.partial(_mix_ffn2_kernel, ts=ts, nt=nt, n_tiles=n_tiles)
    return pl.pallas_call(
        kern,
        grid=(n_tiles + 1,),
        in_specs=[
            pl.BlockSpec((ts, D_MODEL), lambda g: (prev(g), 0)),
            pl.BlockSpec((C_SLAB, ts, LANES), lambda g: (0, cur(g), 0)),
            pl.BlockSpec((C_SLAB, HALO, LANES), lambda g: (0, jnp.maximum(cur(g) * hb - 1, 0), 0)),
            pl.BlockSpec((N_SLAB, ts, LANES), lambda g: (0, cur(g), 0)),
            _const_spec((CONV_WIDTH, D_CONV)),
            _const_spec((1, D_CONV)),
            _const_spec((1, D_CONV)),
            _const_spec((1, D_CONV)),
            _const_spec((1, D_CONV)),
            _const_spec((D_SSM, D_SSM)),
            _const_spec((1, D_SSM)),
            _const_spec((1, D_SSM)),
            _const_spec((D_MODEL, D_MODEL)),
            _const_spec((1, D_MODEL)),
            _const_spec((D_MODEL, D_FF)),
            _const_spec((D_MODEL, D_FF)),
            _const_spec((D_FF, D_MODEL)),
            _const_spec((1, D_MODEL)),
        ],
        out_specs=pl.BlockSpec((ts, D_MODEL), lambda g: (prev(g), 0)),
        out_shape=jax.ShapeDtypeStruct((nb * seq, D_MODEL), F32),
        scratch_shapes=[
            pltpu.VMEM((C_SLAB, HALO + ts, LANES), F32),
            pltpu.VMEM((C_SLAB, ts, LANES), F32),
            pltpu.VMEM((2, ts, D_MODEL), BF16),
            pltpu.VMEM((ts, D_FF), BF16),
        ],
        compiler_params=pltpu.CompilerParams(
            dimension_semantics=("arbitrary",), vmem_limit_bytes=VMEM_LIMIT),
        name="mix_ffn2",
    )(x1, a, a, y4, cw, cb, lng, lnb, cog, gluw, glub, sog, wout, g2, w1, w3, w2, gf)


def kernel(x, norm_ffn1, ffn1_w1, ffn1_w3, ffn1_w2, norm_mix, w_in, conv_w, conv_b, conv_ln_g,
           conv_ln_b, conv_out_g, ssm_A_re, ssm_A_im, ssm_log_dt, ssm_B_re, ssm_B_im, ssm_C_re,
           ssm_C_im, ssm_D, ssm_glu_w, ssm_glu_b, ssm_out_g, w_out, norm_ffn2, ffn2_w1, ffn2_w3,
           ffn2_w2, norm_final):
    nb, seq, _ = x.shape
    depth = norm_ffn1.shape[0]
    assert depth == 1, "the final norm is fused into the last FFN; only depth 1 is laid out"
    tm = 512
    tb = 256
    ts = 256
    assert seq % tb == 0 and seq % ts == 0 and (nb * seq) % tm == 0
    assert tb % (CHUNK * 16) == 0 and ts % HALO == 0 and ts % MIX_ROWS == 0

    row = lambda v: v.reshape(1, -1).astype(F32)
    x2d = x.reshape(nb * seq, D_MODEL)
    for l in range(depth):
        x1, a, u4 = _ffn1_proj(
            x2d, row(norm_ffn1[l]), ffn1_w1[l].astype(BF16), ffn1_w3[l].astype(BF16),
            ffn1_w2[l].astype(BF16), row(norm_mix[l]), w_in[l].astype(BF16), tm)
        wc, vc, al = _s5_compact(
            ssm_A_re[l], ssm_A_im[l], ssm_log_dt[l], ssm_B_re[l], ssm_B_im[l],
            ssm_C_re[l], ssm_C_im[l])
        wmat, tv = _s5_embed(wc, vc, ssm_D[l].reshape(N_SLAB, 1, LANES))
        y4 = _s5_scan(u4.reshape(N_SLAB, nb, seq, LANES), wmat, tv, al, nb, seq, tb)
        x2d = _mix_ffn2(
            x1, a, y4.reshape(N_SLAB, nb * seq, LANES), conv_w[l], row(conv_b[l]),
            row(conv_ln_g[l]), row(conv_ln_b[l]), row(conv_out_g[l]), ssm_glu_w[l].astype(BF16),
            row(ssm_glu_b[l]), row(ssm_out_g[l]), w_out[l].astype(BF16), row(norm_ffn2[l]),
            ffn2_w1[l].astype(BF16), ffn2_w3[l].astype(BF16), ffn2_w2[l].astype(BF16),
            row(norm_final), nb, seq, ts)
    return x2d.reshape(nb, seq, D_MODEL)
```

```python
import functools

import jax
import jax.numpy as jnp
from jax import lax
from jax.experimental import pallas as pl
from jax.experimental.pallas import tpu as pltpu

D_MODEL = 1024
D_CONV = 512
D_SSM = 512
CONV_WIDTH = 31
SSM_GROUP = 16
SSM_GROUPS = 32
SSM_STATE = 64
D_FF = 2816
D_IN = 2 * D_CONV + D_SSM
FFN_RES = 0.5
EPS = 1e-6

LANES = 128
MXU_TILE = 256
SLAB_GROUPS = LANES // SSM_GROUP
N_SLAB = D_SSM // LANES
C_SLAB = D_CONV // LANES
SLAB_STATE = SLAB_GROUPS * SSM_STATE
CHUNK = 8
CHUNK_K = CHUNK * LANES
HALO = 32

VMEM_LIMIT = 56 * 1024 * 1024

F32 = jnp.float32
BF16 = jnp.bfloat16


def _sigmoid(x):
    return 0.5 * jnp.tanh(0.5 * x) + 0.5


def _rmsnorm(x, g):
    ms = jnp.mean(x * x, axis=-1, keepdims=True)
    return x * lax.rsqrt(ms + EPS) * g


def _swiglu_residual(x, g, w1_ref, w3_ref, w2_ref):
    h = _rmsnorm(x, g).astype(BF16)
    a = jnp.dot(h, w1_ref[...], preferred_element_type=F32)
    b = jnp.dot(h, w3_ref[...], preferred_element_type=F32)
    z = (a * _sigmoid(a) * b).astype(BF16)
    o = jnp.dot(z, w2_ref[...], preferred_element_type=F32)
    return x + FFN_RES * o


def _const_spec(shape):
    nd = len(shape)
    return pl.BlockSpec(shape, lambda *_: (0,) * nd, pipeline_mode=pl.Buffered(1))


def _ffn1_proj_kernel(x_ref, g1_ref, w1_ref, w3_ref, w2_ref, gm_ref, win_ref,
                      x1_ref, a_ref, u_ref):
    x1 = _swiglu_residual(x_ref[...], g1_ref[...], w1_ref, w3_ref, w2_ref)
    x1_ref[...] = x1
    h2 = _rmsnorm(x1, gm_ref[...]).astype(BF16)
    proj = jnp.dot(h2, win_ref[...], preferred_element_type=F32)
    for s in range(C_SLAB):
        lo = s * LANES
        a_val = proj[:, lo:lo + LANES]
        a_gate = proj[:, D_CONV + lo:D_CONV + lo + LANES]
        a_ref[s] = a_val * _sigmoid(a_gate)
    for s in range(N_SLAB):
        lo = 2 * D_CONV + s * LANES
        u_ref[s] = proj[:, lo:lo + LANES]


def _ffn1_proj(x2d, g1, w1, w3, w2, gm, win, tm):
    t = x2d.shape[0]
    row = lambda i: (i, 0)
    return pl.pallas_call(
        _ffn1_proj_kernel,
        grid=(t // tm,),
        in_specs=[
            pl.BlockSpec((tm, D_MODEL), row),
            _const_spec((1, D_MODEL)),
            _const_spec((D_MODEL, D_FF)),
            _const_spec((D_MODEL, D_FF)),
            _const_spec((D_FF, D_MODEL)),
            _const_spec((1, D_MODEL)),
            _const_spec((D_MODEL, D_IN)),
        ],
        out_specs=[
            pl.BlockSpec((tm, D_MODEL), row),
            pl.BlockSpec((C_SLAB, tm, LANES), lambda i: (0, i, 0)),
            pl.BlockSpec((N_SLAB, tm, LANES), lambda i: (0, i, 0)),
        ],
        out_shape=[
            jax.ShapeDtypeStruct((t, D_MODEL), F32),
            jax.ShapeDtypeStruct((C_SLAB, t, LANES), F32),
            jax.ShapeDtypeStruct((N_SLAB, t, LANES), F32),
        ],
        compiler_params=pltpu.CompilerParams(
            dimension_semantics=("parallel",), vmem_limit_bytes=VMEM_LIMIT),
        name="ffn1_proj",
    )(x2d, g1, w1, w3, w2, gm, win)


def _s5_compact(a_re, a_im, log_dt, b_re, b_im, c_re, c_im):
    L = CHUNK
    dt = jnp.exp(log_dt)[:, None]
    zr, zi = a_re * dt, a_im * dt
    mag = jnp.exp(zr)
    abar_r, abar_i = mag * jnp.cos(zi), mag * jnp.sin(zi)
    den = a_re * a_re + a_im * a_im
    nr = abar_r - 1.0
    coef_r = (nr * a_re + abar_i * a_im) / den
    coef_i = (abar_i * a_re - nr * a_im) / den
    bb_r = coef_r[..., None] * b_re - coef_i[..., None] * b_im
    bb_i = coef_r[..., None] * b_im + coef_i[..., None] * b_re
    n = jnp.arange(L + 1, dtype=F32)[:, None, None]
    pmag = jnp.exp(n * zr)
    pw_r, pw_i = pmag * jnp.cos(n * zi), pmag * jnp.sin(n * zi)
    cp_r = c_re[None] * pw_r[:, :, None, :] - c_im[None] * pw_i[:, :, None, :]
    cp_i = c_re[None] * pw_i[:, :, None, :] + c_im[None] * pw_r[:, :, None, :]
    vc = jnp.concatenate([cp_r, -cp_i], axis=-1)
    nrev = (L - 1) - n[:L]
    rmag = jnp.exp(nrev * zr)
    rv_r = (rmag * jnp.cos(nrev * zi))[:, :, None, :]
    rv_i = (rmag * jnp.sin(nrev * zi))[:, :, None, :]
    bt_r, bt_i = bb_r.transpose(0, 2, 1)[None], bb_i.transpose(0, 2, 1)[None]
    wc = jnp.concatenate([rv_r * bt_r - rv_i * bt_i, rv_r * bt_i + rv_i * bt_r], axis=-1)

    def per_slab(v):
        v = v.reshape(v.shape[0], N_SLAB, LANES, 2 * SSM_STATE)
        return v.transpose(1, 0, 2, 3).reshape(N_SLAB, -1, 2 * SSM_STATE)

    al = jnp.stack([pw_r[L].reshape(N_SLAB, SLAB_STATE), pw_i[L].reshape(N_SLAB, SLAB_STATE)], axis=1)
    return per_slab(wc), per_slab(vc), al


def _s5_embed_kernel(wc_ref, vc_ref, d_ref, w_ref, tv_ref):
    L = CHUNK

    def embed(c):
        rows = c.shape[0]
        row_group = (lax.broadcasted_iota(jnp.int32, (rows, LANES), 0) // SSM_GROUP) % SLAB_GROUPS
        lane_half = lax.broadcasted_iota(jnp.int32, (rows, LANES), 1) // SSM_STATE
        pieces = []
        for part in range(2):
            x = c[:, part * SSM_STATE:(part + 1) * SSM_STATE]
            xx = jnp.concatenate([x] * (LANES // SSM_STATE), axis=1)
            for k in range(SLAB_STATE // LANES):
                own = row_group == (LANES // SSM_STATE) * k + lane_half
                pieces.append(jnp.where(own, xx, 0.0))
        return jnp.concatenate(pieces, axis=1)

    w = embed(wc_ref[...])
    vt = embed(vc_ref[...])
    w_ref[...] = w.astype(BF16)

    w0 = w[(L - 1) * LANES:, :]
    nt = (((1,), (1,)), ((), ()))
    ri = lax.broadcasted_iota(jnp.int32, (LANES, LANES), 0)
    ci = lax.broadcasted_iota(jnp.int32, (LANES, LANES), 1)
    zero = jnp.zeros((LANES, LANES), BF16)
    for n in range(L):
        k_n = lax.dot_general(w0, vt[n * LANES:(n + 1) * LANES, :], nt,
                              precision=lax.Precision.HIGHEST, preferred_element_type=F32)
        if n == 0:
            k_n = k_n + jnp.where(ri == ci, d_ref[...], 0.0)
        k_n = k_n.astype(BF16)
        for s in range(L - n):
            t = s + n
            tv_ref[s * LANES:(s + 1) * LANES, t * LANES:(t + 1) * LANES] = k_n
    for s in range(1, L):
        for t in range(s):
            tv_ref[s * LANES:(s + 1) * LANES, t * LANES:(t + 1) * LANES] = zero
    tv_ref[L * LANES:, :] = vt[LANES:, :].T.astype(BF16)


def _s5_embed(wc, vc, d4):
    n_state = 2 * SLAB_STATE
    per_slab = lambda s: (s, 0, 0)
    return pl.pallas_call(
        _s5_embed_kernel,
        grid=(N_SLAB,),
        in_specs=[
            pl.BlockSpec((None, CHUNK_K, 2 * SSM_STATE), per_slab),
            pl.BlockSpec((None, CHUNK_K + LANES, 2 * SSM_STATE), per_slab),
            pl.BlockSpec((None, 1, LANES), per_slab),
        ],
        out_specs=[
            pl.BlockSpec((None, CHUNK_K, n_state), per_slab),
            pl.BlockSpec((None, CHUNK_K + n_state, CHUNK_K), per_slab),
        ],
        out_shape=[
            jax.ShapeDtypeStruct((N_SLAB, CHUNK_K, n_state), BF16),
            jax.ShapeDtypeStruct((N_SLAB, CHUNK_K + n_state, CHUNK_K), BF16),
        ],
        compiler_params=pltpu.CompilerParams(
            dimension_semantics=("parallel",), vmem_limit_bytes=VMEM_LIMIT),
        name="s5_embed",
    )(wc, vc, d4)


def _s5_kernel(u_ref, w_ref, tv_ref, al_ref, y_ref, lhs_ref, xpb_ref, z_ref, xp_ref, st_ref,
               *, nb, mc):
    n_state_slab = 2 * SLAB_STATE // LANES
    half = n_state_slab // 2
    pitch = mc + 8

    @pl.when(pl.program_id(1) == 0)
    def _():
        st_ref[...] = jnp.zeros_like(st_ref)

    for b in range(nb):
        for s in range(CHUNK):
            piece = u_ref[b, pl.ds(s, mc, stride=CHUNK), :]
            lhs_ref[b * mc:(b + 1) * mc, s * LANES:(s + 1) * LANES] = piece.astype(BF16)

    z = jnp.dot(lhs_ref[...], w_ref[...], preferred_element_type=F32)
    for b in range(nb):
        for k in range(n_state_slab):
            z_ref[k, b * pitch:b * pitch + mc, :] = z[b * mc:(b + 1) * mc, k * LANES:(k + 1) * LANES]

    y_cols = []
    for n in range(CHUNK_K // MXU_TILE):
        kk = (n + 1) * MXU_TILE
        y_cols.append(jnp.dot(lhs_ref[:, :kk], tv_ref[:kk, n * MXU_TILE:(n + 1) * MXU_TILE],
                              preferred_element_type=F32))

    ar = al_ref[0:1, :]
    ai = al_ref[1:2, :]
    xr = st_ref[:, :SLAB_STATE]
    xi = st_ref[:, SLAB_STATE:]
    for m in range(mc):
        rows = pl.ds(m, nb, stride=pitch)
        for k in range(half):
            xp_ref[k, rows, :] = xr[:, k * LANES:(k + 1) * LANES]
            xp_ref[half + k, rows, :] = xi[:, k * LANES:(k + 1) * LANES]
        zr = jnp.concatenate([z_ref[k, rows, :] for k in range(half)], axis=1)
        zi = jnp.concatenate([z_ref[half + k, rows, :] for k in range(half)], axis=1)
        xr, xi = ar * xr - ai * xi + zr, ar * xi + ai * xr + zi
    st_ref[:, :SLAB_STATE] = xr
    st_ref[:, SLAB_STATE:] = xi

    for b in range(nb):
        for k in range(n_state_slab):
            xpb_ref[b * mc:(b + 1) * mc, k * LANES:(k + 1) * LANES] = (
                xp_ref[k, b * pitch:b * pitch + mc, :].astype(BF16))

    y = (jnp.concatenate(y_cols, axis=1)
         + jnp.dot(xpb_ref[...], tv_ref[CHUNK_K:, :], preferred_element_type=F32))
    for b in range(nb):
        for t in range(CHUNK):
            y_ref[b, pl.ds(t, mc, stride=CHUNK), :] = y[b * mc:(b + 1) * mc, t * LANES:(t + 1) * LANES]


def _s5_scan(u4, wmat, tv, al, nb, seq, tb):
    mc = tb // CHUNK
    rows = nb * mc
    n_state = 2 * SLAB_STATE
    kern = functools.partial(_s5_kernel, nb=nb, mc=mc)
    blk = lambda s, j: (s, 0, j, 0)
    per_slab = lambda s, j: (s, 0, 0)
    return pl.pallas_call(
        kern,
        grid=(N_SLAB, seq // tb),
        in_specs=[
            pl.BlockSpec((None, nb, tb, LANES), blk),
            pl.BlockSpec((None, CHUNK_K, n_state), per_slab),
            pl.BlockSpec((None, CHUNK_K + n_state, CHUNK_K), per_slab),
            pl.BlockSpec((None, 2, SLAB_STATE), per_slab),
        ],
        out_specs=pl.BlockSpec((None, nb, tb, LANES), blk),
        out_shape=jax.ShapeDtypeStruct((N_SLAB, nb, seq, LANES), F32),
        scratch_shapes=[
            pltpu.VMEM((rows, CHUNK_K), BF16),
            pltpu.VMEM((rows, n_state), BF16),
            pltpu.VMEM((n_state // LANES, nb * (mc + 8), LANES), F32),
            pltpu.VMEM((n_state // LANES, nb * (mc + 8), LANES), F32),
            pltpu.VMEM((nb, n_state), F32),
        ],
        compiler_params=pltpu.CompilerParams(
            dimension_semantics=("arbitrary", "arbitrary"), vmem_limit_bytes=VMEM_LIMIT),
        name="s5_scan",
    )(u4, wmat, tv, al)


CONV_ROWS = 16
CONV_STRIDE = 4
MIX_ROWS = CONV_ROWS * CONV_STRIDE


def _order_token(v):
    t = v[:, :LANES]
    for lo in range(LANES, v.shape[1], LANES):
        t = jnp.maximum(t, v[:, lo:lo + LANES])
    return jnp.max(t, axis=0, keepdims=True)


def _wait_for(x, tokens):
    for t in tokens:
        bits = pltpu.bitcast(t, jnp.uint32)
        bits = lax.shift_right_logical(lax.shift_right_logical(bits, jnp.uint32(16)), jnp.uint32(16))
        x = pltpu.bitcast(pltpu.bitcast(x, jnp.uint32) + bits, F32)
    return x


def _conv_chain(r0, s, ph, after, cw_ref, cb_ref, aext_ref, conv_ref):
    base = HALO - (CONV_WIDTH - 1)
    lanes = slice(s * LANES, (s + 1) * LANES)
    acc = _wait_for(jnp.broadcast_to(cb_ref[:, lanes], (CONV_ROWS, LANES)), after)
    for k in range(CONV_WIDTH):
        rows = pl.ds(r0 + ph + base + k, CONV_ROWS, stride=CONV_STRIDE)
        acc = acc + cw_ref[k:k + 1, lanes] * aext_ref[s, rows, :]
    conv_ref[s, pl.ds(r0 + ph, CONV_ROWS, stride=CONV_STRIDE), :] = acc
    return _order_token(acc)


def _mix_rows(r0, conv_ref, y4_ref, lng_ref, lnb_ref, cog_ref, gluw_ref, glub_ref, sog_ref, dst_ref):
    rows = slice(r0, r0 + MIX_ROWS)
    c = jnp.concatenate([conv_ref[s, rows, :] for s in range(C_SLAB)], axis=1)
    mu = jnp.mean(c, axis=-1, keepdims=True)
    xc = c - mu
    var = jnp.mean(xc * xc, axis=-1, keepdims=True)
    c = xc * lax.rsqrt(var + EPS) * lng_ref[...] + lnb_ref[...]
    c = c * jax.nn.sigmoid(c)
    a_out = _rmsnorm(c, cog_ref[...])

    y = jnp.concatenate([y4_ref[s, rows, :] for s in range(N_SLAB)], axis=1)
    yg = jax.nn.gelu(y)
    gate = jnp.dot(yg.astype(BF16), gluw_ref[...], preferred_element_type=F32) + glub_ref[...]
    s_out = _rmsnorm(yg * jax.nn.sigmoid(gate), sog_ref[...])
    dst_ref[rows, :] = jnp.concatenate([a_out, s_out], axis=1).astype(BF16)


def _mix_ffn2_kernel(x1_ref, a_ref, halo_ref, y4_ref, cw_ref, cb_ref, lng_ref, lnb_ref, cog_ref,
                     gluw_ref, glub_ref, sog_ref, wout_ref, g2_ref, w1_ref, w3_ref, w2_ref, gf_ref,
                     o_ref, aext_ref, conv_ref, mixed_ref, z_ref, *, ts, nt, n_tiles):
    g = pl.program_id(0)

    @pl.when(g == 0)
    def _():
        mixed_ref[...] = jnp.zeros_like(mixed_ref)

    first = jnp.minimum(g, n_tiles - 1) % nt == 0
    aext_ref[:, 0:HALO, :] = jnp.where(first, 0.0, halo_ref[...])
    aext_ref[:, HALO:, :] = a_ref[...]

    conv_after = []

    def conv_piece(r0, s, ph):
        conv_after[:] = [_conv_chain(r0, s, ph, conv_after, cw_ref, cb_ref, aext_ref, conv_ref)]

    def rows_piece(r0):
        _mix_rows(r0, conv_ref, y4_ref, lng_ref, lnb_ref, cog_ref, gluw_ref, glub_ref, sog_ref,
                  mixed_ref.at[g % 2])

    mix_pieces = []
    for r0 in range(0, ts, MIX_ROWS):
        for s in range(C_SLAB):
            for ph in range(CONV_STRIDE):
                mix_pieces.append(functools.partial(conv_piece, r0, s, ph))
        mix_pieces.append(functools.partial(rows_piece, r0))
    n_out = D_MODEL // MXU_TILE
    n_ff = D_FF // MXU_TILE
    done = [0]

    def fill(slot, result):
        upto = (len(mix_pieces) * (slot + 1)) // n_ff
        conv_after.append(_order_token(result))
        for piece in mix_pieces[done[0]:upto]:
            piece()
        done[0] = upto

    col = lambda n: slice(n * MXU_TILE, (n + 1) * MXU_TILE)
    mixed_prev = mixed_ref[(g + 1) % 2]
    x2_cols = [x1_ref[:, col(n)]
               + jnp.dot(mixed_prev, wout_ref[:, col(n)], preferred_element_type=F32)
               for n in range(n_out)]
    x2 = jnp.concatenate(x2_cols, axis=1)
    h = _rmsnorm(x2, g2_ref[...]).astype(BF16)
    for n in range(n_ff):
        a = jnp.dot(h, w1_ref[:, col(n)], preferred_element_type=F32)
        b = jnp.dot(h, w3_ref[:, col(n)], preferred_element_type=F32)
        zf = a * jax.nn.sigmoid(a) * b
        z_ref[:, col(n)] = zf.astype(BF16)
        fill(n, zf)
    z = z_ref[...]
    x3_cols = [x2_cols[n] + FFN_RES * jnp.dot(z, w2_ref[:, col(n)], preferred_element_type=F32)
               for n in range(n_out)]
    o_ref[...] = _rmsnorm(jnp.concatenate(x3_cols, axis=1), gf_ref[...])


def _mix_ffn2(x1, a, y4, cw, cb, lng, lnb, cog, gluw, glub, sog, wout, g2, w1, w3, w2, gf,
              nb, seq, ts):
    nt = seq // ts
    n_tiles = nb * nt
    hb = ts // HALO
    cur = lambda g: jnp.minimum(g, n_tiles - 1)
    prev = lambda g: jnp.maximum(g - 1, 0)
    kern = functools.partial(_mix_ffn2_kernel, ts=ts, nt=nt, n_tiles=n_tiles)
    return pl.pallas_call(
        kern,
        grid=(n_tiles + 1,),
        in_specs=[
            pl.BlockSpec((ts, D_MODEL), lambda g: (prev(g), 0)),
            pl.BlockSpec((C_SLAB, ts, LANES), lambda g: (0, cur(g), 0)),
            pl.BlockSpec((C_SLAB, HALO, LANES), lambda g: (0, jnp.maximum(cur(g) * hb - 1, 0), 0)),
            pl.BlockSpec((N_SLAB, ts, LANES), lambda g: (0, cur(g), 0)),
            _const_spec((CONV_WIDTH, D_CONV)),
            _const_spec((1, D_CONV)),
            _const_spec((1, D_CONV)),
            _const_spec((1, D_CONV)),
            _const_spec((1, D_CONV)),
            _const_spec((D_SSM, D_SSM)),
            _const_spec((1, D_SSM)),
            _const_spec((1, D_SSM)),
            _const_spec((D_MODEL, D_MODEL)),
            _const_spec((1, D_MODEL)),
            _const_spec((D_MODEL, D_FF)),
            _const_spec((D_MODEL, D_FF)),
            _const_spec((D_FF, D_MODEL)),
            _const_spec((1, D_MODEL)),
        ],
        out_specs=pl.BlockSpec((ts, D_MODEL), lambda g: (prev(g), 0)),
        out_shape=jax.ShapeDtypeStruct((nb * seq, D_MODEL), F32),
        scratch_shapes=[
            pltpu.VMEM((C_SLAB, HALO + ts, LANES), F32),
            pltpu.VMEM((C_SLAB, ts, LANES), F32),
            pltpu.VMEM((2, ts, D_MODEL), BF16),
            pltpu.VMEM((ts, D_FF), BF16),
        ],
        compiler_params=pltpu.CompilerParams(
            dimension_semantics=("arbitrary",), vmem_limit_bytes=VMEM_LIMIT),
        name="mix_ffn2",
    )(x1, a, a, y4, cw, cb, lng, lnb, cog, gluw, glub, sog, wout, g2, w1, w3, w2, gf)


def kernel(x, norm_ffn1, ffn1_w1, ffn1_w3, ffn1_w2, norm_mix, w_in, conv_w, conv_b, conv_ln_g,
           conv_ln_b, conv_out_g, ssm_A_re, ssm_A_im, ssm_log_dt, ssm_B_re, ssm_B_im, ssm_C_re,
           ssm_C_im, ssm_D, ssm_glu_w, ssm_glu_b, ssm_out_g, w_out, norm_ffn2, ffn2_w1, ffn2_w3,
           ffn2_w2, norm_final):
    nb, seq, _ = x.shape
    depth = norm_ffn1.shape[0]
    assert depth == 1, "the final norm is fused into the last FFN; only depth 1 is laid out"
    tm = 512
    tb = 512
    ts = 512
    assert seq % tb == 0 and seq % ts == 0 and (nb * seq) % tm == 0
    assert tb % (CHUNK * 16) == 0 and ts % HALO == 0 and ts % MIX_ROWS == 0

    row = lambda v: v.reshape(1, -1).astype(F32)
    x2d = x.reshape(nb * seq, D_MODEL)
    for l in range(depth):
        x1, a, u4 = _ffn1_proj(
            x2d, row(norm_ffn1[l]), ffn1_w1[l].astype(BF16), ffn1_w3[l].astype(BF16),
            ffn1_w2[l].astype(BF16), row(norm_mix[l]), w_in[l].astype(BF16), tm)
        wc, vc, al = _s5_compact(
            ssm_A_re[l], ssm_A_im[l], ssm_log_dt[l], ssm_B_re[l], ssm_B_im[l],
            ssm_C_re[l], ssm_C_im[l])
        wmat, tv = _s5_embed(wc, vc, ssm_D[l].reshape(N_SLAB, 1, LANES))
        y4 = _s5_scan(u4.reshape(N_SLAB, nb, seq, LANES), wmat, tv, al, nb, seq, tb)
        x2d = _mix_ffn2(
            x1, a, y4.reshape(N_SLAB, nb * seq, LANES), conv_w[l], row(conv_b[l]),
            row(conv_ln_g[l]), row(conv_ln_b[l]), row(conv_out_g[l]), ssm_glu_w[l].astype(BF16),
            row(ssm_glu_b[l]), row(ssm_out_g[l]), w_out[l].astype(BF16), row(norm_ffn2[l]),
            ffn2_w1[l].astype(BF16), ffn2_w3[l].astype(BF16), ffn2_w2[l].astype(BF16),
            row(norm_final), nb, seq, ts)
    return x2d.reshape(nb, seq, D_MODEL)
```

```python
import functools

import jax
import jax.numpy as jnp
from jax import lax
from jax.experimental import pallas as pl
from jax.experimental.pallas import tpu as pltpu

D_MODEL = 1024
D_CONV = 512
D_SSM = 512
CONV_WIDTH = 31
SSM_GROUP = 16
SSM_GROUPS = 32
SSM_STATE = 64
D_FF = 2816
D_IN = 2 * D_CONV + D_SSM
FFN_RES = 0.5
EPS = 1e-6

LANES = 128
MXU_TILE = 256
SLAB_GROUPS = LANES // SSM_GROUP
N_SLAB = D_SSM // LANES
C_SLAB = D_CONV // LANES
SLAB_STATE = SLAB_GROUPS * SSM_STATE
CHUNK = 8
CHUNK_K = CHUNK * LANES
HALO = 32

VMEM_LIMIT = 56 * 1024 * 1024

F32 = jnp.float32
BF16 = jnp.bfloat16


def _sigmoid(x):
    return 0.5 * jnp.tanh(0.5 * x) + 0.5


def _rmsnorm(x, g):
    ms = jnp.mean(x * x, axis=-1, keepdims=True)
    return x * lax.rsqrt(ms + EPS) * g


def _swiglu_residual(x, g, w1_ref, w3_ref, w2_ref):
    h = _rmsnorm(x, g).astype(BF16)
    a = jnp.dot(h, w1_ref[...], preferred_element_type=F32)
    b = jnp.dot(h, w3_ref[...], preferred_element_type=F32)
    z = (a * _sigmoid(a) * b).astype(BF16)
    o = jnp.dot(z, w2_ref[...], preferred_element_type=F32)
    return x + FFN_RES * o


def _const_spec(shape):
    nd = len(shape)
    return pl.BlockSpec(shape, lambda *_: (0,) * nd, pipeline_mode=pl.Buffered(1))


def _ffn1_proj_kernel(x_ref, g1_ref, w1_ref, w3_ref, w2_ref, gm_ref, win_ref,
                      x1_ref, a_ref, u_ref):
    x1 = _swiglu_residual(x_ref[...], g1_ref[...], w1_ref, w3_ref, w2_ref)
    x1_ref[...] = x1
    h2 = _rmsnorm(x1, gm_ref[...]).astype(BF16)
    proj = jnp.dot(h2, win_ref[...], preferred_element_type=F32)
    for s in range(C_SLAB):
        lo = s * LANES
        a_val = proj[:, lo:lo + LANES]
        a_gate = proj[:, D_CONV + lo:D_CONV + lo + LANES]
        a_ref[s] = a_val * _sigmoid(a_gate)
    for s in range(N_SLAB):
        lo = 2 * D_CONV + s * LANES
        u_ref[s] = proj[:, lo:lo + LANES]


def _ffn1_proj(x2d, g1, w1, w3, w2, gm, win, tm):
    t = x2d.shape[0]
    row = lambda i: (i, 0)
    return pl.pallas_call(
        _ffn1_proj_kernel,
        grid=(t // tm,),
        in_specs=[
            pl.BlockSpec((tm, D_MODEL), row),
            _const_spec((1, D_MODEL)),
            _const_spec((D_MODEL, D_FF)),
            _const_spec((D_MODEL, D_FF)),
            _const_spec((D_FF, D_MODEL)),
            _const_spec((1, D_MODEL)),
            _const_spec((D_MODEL, D_IN)),
        ],
        out_specs=[
            pl.BlockSpec((tm, D_MODEL), row),
            pl.BlockSpec((C_SLAB, tm, LANES), lambda i: (0, i, 0)),
            pl.BlockSpec((N_SLAB, tm, LANES), lambda i: (0, i, 0)),
        ],
        out_shape=[
            jax.ShapeDtypeStruct((t, D_MODEL), F32),
            jax.ShapeDtypeStruct((C_SLAB, t, LANES), F32),
            jax.ShapeDtypeStruct((N_SLAB, t, LANES), F32),
        ],
        compiler_params=pltpu.CompilerParams(
            dimension_semantics=("parallel",), vmem_limit_bytes=VMEM_LIMIT),
        name="ffn1_proj",
    )(x2d, g1, w1, w3, w2, gm, win)


def _s5_compact(a_re, a_im, log_dt, b_re, b_im, c_re, c_im):
    L = CHUNK
    dt = jnp.exp(log_dt)[:, None]
    zr, zi = a_re * dt, a_im * dt
    mag = jnp.exp(zr)
    abar_r, abar_i = mag * jnp.cos(zi), mag * jnp.sin(zi)
    den = a_re * a_re + a_im * a_im
    nr = abar_r - 1.0
    coef_r = (nr * a_re + abar_i * a_im) / den
    coef_i = (abar_i * a_re - nr * a_im) / den
    bb_r = coef_r[..., None] * b_re - coef_i[..., None] * b_im
    bb_i = coef_r[..., None] * b_im + coef_i[..., None] * b_re
    n = jnp.arange(L + 1, dtype=F32)[:, None, None]
    pmag = jnp.exp(n * zr)
    pw_r, pw_i = pmag * jnp.cos(n * zi), pmag * jnp.sin(n * zi)
    cp_r = c_re[None] * pw_r[:, :, None, :] - c_im[None] * pw_i[:, :, None, :]
    cp_i = c_re[None] * pw_i[:, :, None, :] + c_im[None] * pw_r[:, :, None, :]
    vc = jnp.concatenate([cp_r, -cp_i], axis=-1)
    nrev = (L - 1) - n[:L]
    rmag = jnp.exp(nrev * zr)
    rv_r = (rmag * jnp.cos(nrev * zi))[:, :, None, :]
    rv_i = (rmag * jnp.sin(nrev * zi))[:, :, None, :]
    bt_r, bt_i = bb_r.transpose(0, 2, 1)[None], bb_i.transpose(0, 2, 1)[None]
    wc = jnp.concatenate([rv_r * bt_r - rv_i * bt_i, rv_r * bt_i + rv_i * bt_r], axis=-1)

    def per_slab(v):
        v = v.reshape(v.shape[0], N_SLAB, LANES, 2 * SSM_STATE)
        return v.transpose(1, 0, 2, 3).reshape(N_SLAB, -1, 2 * SSM_STATE)

    al = jnp.stack([pw_r[L].reshape(N_SLAB, SLAB_STATE), pw_i[L].reshape(N_SLAB, SLAB_STATE)], axis=1)
    return per_slab(wc), per_slab(vc), al


def _s5_embed_kernel(wc_ref, vc_ref, d_ref, w_ref, tv_ref):
    L = CHUNK

    def embed(c):
        rows = c.shape[0]
        row_group = (lax.broadcasted_iota(jnp.int32, (rows, LANES), 0) // SSM_GROUP) % SLAB_GROUPS
        lane_half = lax.broadcasted_iota(jnp.int32, (rows, LANES), 1) // SSM_STATE
        pieces = []
        for part in range(2):
            x = c[:, part * SSM_STATE:(part + 1) * SSM_STATE]
            xx = jnp.concatenate([x] * (LANES // SSM_STATE), axis=1)
            for k in range(SLAB_STATE // LANES):
                own = row_group == (LANES // SSM_STATE) * k + lane_half
                pieces.append(jnp.where(own, xx, 0.0))
        return jnp.concatenate(pieces, axis=1)

    w = embed(wc_ref[...])
    vt = embed(vc_ref[...])
    w_ref[...] = w.astype(BF16)

    w0 = w[(L - 1) * LANES:, :]
    nt = (((1,), (1,)), ((), ()))
    ri = lax.broadcasted_iota(jnp.int32, (LANES, LANES), 0)
    ci = lax.broadcasted_iota(jnp.int32, (LANES, LANES), 1)
    zero = jnp.zeros((LANES, LANES), BF16)
    for n in range(L):
        k_n = lax.dot_general(w0, vt[n * LANES:(n + 1) * LANES, :], nt,
                              precision=lax.Precision.HIGHEST, preferred_element_type=F32)
        if n == 0:
            k_n = k_n + jnp.where(ri == ci, d_ref[...], 0.0)
        k_n = k_n.astype(BF16)
        for s in range(L - n):
            t = s + n
            tv_ref[s * LANES:(s + 1) * LANES, t * LANES:(t + 1) * LANES] = k_n
    for s in range(1, L):
        for t in range(s):
            tv_ref[s * LANES:(s + 1) * LANES, t * LANES:(t + 1) * LANES] = zero
    tv_ref[L * LANES:, :] = vt[LANES:, :].T.astype(BF16)


def _s5_embed(wc, vc, d4):
    n_state = 2 * SLAB_STATE
    per_slab = lambda s: (s, 0, 0)
    return pl.pallas_call(
        _s5_embed_kernel,
        grid=(N_SLAB,),
        in_specs=[
            pl.BlockSpec((None, CHUNK_K, 2 * SSM_STATE), per_slab),
            pl.BlockSpec((None, CHUNK_K + LANES, 2 * SSM_STATE), per_slab),
            pl.BlockSpec((None, 1, LANES), per_slab),
        ],
        out_specs=[
            pl.BlockSpec((None, CHUNK_K, n_state), per_slab),
            pl.BlockSpec((None, CHUNK_K + n_state, CHUNK_K), per_slab),
        ],
        out_shape=[
            jax.ShapeDtypeStruct((N_SLAB, CHUNK_K, n_state), BF16),
            jax.ShapeDtypeStruct((N_SLAB, CHUNK_K + n_state, CHUNK_K), BF16),
        ],
        compiler_params=pltpu.CompilerParams(
            dimension_semantics=("parallel",), vmem_limit_bytes=VMEM_LIMIT),
        name="s5_embed",
    )(wc, vc, d4)


def _s5_kernel(u_ref, w_ref, tv_ref, al_ref, y_ref, lhs_ref, xpb_ref, z_ref, xp_ref, st_ref,
               *, nb, mc):
    n_state_slab = 2 * SLAB_STATE // LANES
    half = n_state_slab // 2
    pitch = mc + 8

    @pl.when(pl.program_id(1) == 0)
    def _():
        st_ref[...] = jnp.zeros_like(st_ref)

    for b in range(nb):
        for s in range(CHUNK):
            piece = u_ref[b, pl.ds(s, mc, stride=CHUNK), :]
            lhs_ref[b * mc:(b + 1) * mc, s * LANES:(s + 1) * LANES] = piece.astype(BF16)

    z = jnp.dot(lhs_ref[...], w_ref[...], preferred_element_type=F32)
    for b in range(nb):
        for k in range(n_state_slab):
            z_ref[k, b * pitch:b * pitch + mc, :] = z[b * mc:(b + 1) * mc, k * LANES:(k + 1) * LANES]

    y_cols = []
    for n in range(CHUNK_K // MXU_TILE):
        kk = (n + 1) * MXU_TILE
        y_cols.append(jnp.dot(lhs_ref[:, :kk], tv_ref[:kk, n * MXU_TILE:(n + 1) * MXU_TILE],
                              preferred_element_type=F32))

    ar = al_ref[0:1, :]
    ai = al_ref[1:2, :]
    xr = st_ref[:, :SLAB_STATE]
    xi = st_ref[:, SLAB_STATE:]
    for m in range(mc):
        rows = pl.ds(m, nb, stride=pitch)
        for k in range(half):
            xp_ref[k, rows, :] = xr[:, k * LANES:(k + 1) * LANES]
            xp_ref[half + k, rows, :] = xi[:, k * LANES:(k + 1) * LANES]
        zr = jnp.concatenate([z_ref[k, rows, :] for k in range(half)], axis=1)
        zi = jnp.concatenate([z_ref[half + k, rows, :] for k in range(half)], axis=1)
        xr, xi = ar * xr - ai * xi + zr, ar * xi + ai * xr + zi
    st_ref[:, :SLAB_STATE] = xr
    st_ref[:, SLAB_STATE:] = xi

    for b in range(nb):
        for k in range(n_state_slab):
            xpb_ref[b * mc:(b + 1) * mc, k * LANES:(k + 1) * LANES] = (
                xp_ref[k, b * pitch:b * pitch + mc, :].astype(BF16))

    y = (jnp.concatenate(y_cols, axis=1)
         + jnp.dot(xpb_ref[...], tv_ref[CHUNK_K:, :], preferred_element_type=F32))
    for b in range(nb):
        for t in range(CHUNK):
            y_ref[b, pl.ds(t, mc, stride=CHUNK), :] = y[b * mc:(b + 1) * mc, t * LANES:(t + 1) * LANES]


def _s5_scan(u4, wmat, tv, al, nb, seq, tb):
    mc = tb // CHUNK
    rows = nb * mc
    n_state = 2 * SLAB_STATE
    kern = functools.partial(_s5_kernel, nb=nb, mc=mc)
    blk = lambda s, j: (s, 0, j, 0)
    per_slab = lambda s, j: (s, 0, 0)
    return pl.pallas_call(
        kern,
        grid=(N_SLAB, seq // tb),
        in_specs=[
            pl.BlockSpec((None, nb, tb, LANES), blk),
            pl.BlockSpec((None, CHUNK_K, n_state), per_slab),
            pl.BlockSpec((None, CHUNK_K + n_state, CHUNK_K), per_slab),
            pl.BlockSpec((None, 2, SLAB_STATE), per_slab),
        ],
        out_specs=pl.BlockSpec((None, nb, tb, LANES), blk),
        out_shape=jax.ShapeDtypeStruct((N_SLAB, nb, seq, LANES), F32),
        scratch_shapes=[
            pltpu.VMEM((rows, CHUNK_K), BF16),
            pltpu.VMEM((rows, n_state), BF16),
            pltpu.VMEM((n_state // LANES, nb * (mc + 8), LANES), F32),
            pltpu.VMEM((n_state // LANES, nb * (mc + 8), LANES), F32),
            pltpu.VMEM((nb, n_state), F32),
        ],
        compiler_params=pltpu.CompilerParams(
            dimension_semantics=("arbitrary", "arbitrary"), vmem_limit_bytes=VMEM_LIMIT),
        name="s5_scan",
    )(u4, wmat, tv, al)


def _s5_out_kernel(y4_ref, gluw_ref, glub_ref, sog_ref, o_ref):
    y = jnp.concatenate([y4_ref[s] for s in range(N_SLAB)], axis=1)
    yg = jax.nn.gelu(y)
    gate = jnp.dot(yg.astype(BF16), gluw_ref[...], preferred_element_type=F32) + glub_ref[...]
    o_ref[...] = _rmsnorm(yg * _sigmoid(gate), sog_ref[...]).astype(BF16)


def _s5_out(y4, gluw, glub, sog, tr):
    t = y4.shape[1]
    return pl.pallas_call(
        _s5_out_kernel,
        grid=(t // tr,),
        in_specs=[
            pl.BlockSpec((N_SLAB, tr, LANES), lambda i: (0, i, 0)),
            _const_spec((D_SSM, D_SSM)),
            _const_spec((1, D_SSM)),
            _const_spec((1, D_SSM)),
        ],
        out_specs=pl.BlockSpec((tr, D_SSM), lambda i: (i, 0)),
        out_shape=jax.ShapeDtypeStruct((t, D_SSM), BF16),
        compiler_params=pltpu.CompilerParams(
            dimension_semantics=("parallel",), vmem_limit_bytes=VMEM_LIMIT),
        name="s5_out",
    )(y4, gluw, glub, sog)


CONV_ROWS = 32
CONV_STRIDE = 4
MIX_ROWS = CONV_ROWS * CONV_STRIDE


def _order_token(v):
    t = v[:, :LANES]
    for lo in range(LANES, v.shape[1], LANES):
        t = jnp.maximum(t, v[:, lo:lo + LANES])
    return jnp.max(t, axis=0, keepdims=True)


def _wait_for(x, tokens):
    for t in tokens:
        bits = pltpu.bitcast(t, jnp.uint32)
        bits = lax.shift_right_logical(lax.shift_right_logical(bits, jnp.uint32(16)), jnp.uint32(16))
        x = pltpu.bitcast(pltpu.bitcast(x, jnp.uint32) + bits, F32)
    return x


def _conv_chain(r0, s, ph, after, cw_ref, cb_ref, aext_ref, conv_ref):
    base = HALO - (CONV_WIDTH - 1)
    lanes = slice(s * LANES, (s + 1) * LANES)
    acc = _wait_for(jnp.broadcast_to(cb_ref[:, lanes], (CONV_ROWS, LANES)), after)
    for k in range(CONV_WIDTH):
        rows = pl.ds(r0 + ph + base + k, CONV_ROWS, stride=CONV_STRIDE)
        acc = acc + cw_ref[k:k + 1, lanes] * aext_ref[s, rows, :]
    conv_ref[s, pl.ds(r0 + ph, CONV_ROWS, stride=CONV_STRIDE), :] = acc
    return _order_token(acc)


def _mix_rows(r0, conv_ref, s_ref, lng_ref, lnb_ref, cog_ref, dst_ref):
    rows = slice(r0, r0 + MIX_ROWS)
    c = jnp.concatenate([conv_ref[s, rows, :] for s in range(C_SLAB)], axis=1)
    mu = jnp.mean(c, axis=-1, keepdims=True)
    xc = c - mu
    var = jnp.mean(xc * xc, axis=-1, keepdims=True)
    c = xc * lax.rsqrt(var + EPS) * lng_ref[...] + lnb_ref[...]
    c = c * jax.nn.sigmoid(c)
    a_out = _rmsnorm(c, cog_ref[...])
    dst_ref[rows, :] = jnp.concatenate([a_out.astype(BF16), s_ref[rows, :]], axis=1)


def _mix_ffn2_kernel(x1_ref, a_ref, halo_ref, s_ref, cw_ref, cb_ref, lng_ref, lnb_ref, cog_ref,
                     wout_ref, g2_ref, w1_ref, w3_ref, w2_ref, gf_ref,
                     o_ref, aext_ref, conv_ref, mixed_ref, z_ref, *, ts, nt, n_tiles):
    g = pl.program_id(0)

    @pl.when(g == 0)
    def _():
        mixed_ref[...] = jnp.zeros_like(mixed_ref)

    first = jnp.minimum(g, n_tiles - 1) % nt == 0
    aext_ref[:, 0:HALO, :] = jnp.where(first, 0.0, halo_ref[...])
    aext_ref[:, HALO:, :] = a_ref[...]

    conv_after = []

    def conv_piece(r0, s, ph):
        conv_after[:] = [_conv_chain(r0, s, ph, conv_after, cw_ref, cb_ref, aext_ref, conv_ref)]

    def rows_piece(r0):
        _mix_rows(r0, conv_ref, s_ref, lng_ref, lnb_ref, cog_ref, mixed_ref.at[g % 2])

    mix_pieces = []
    for r0 in range(0, ts, MIX_ROWS):
        for s in range(C_SLAB):
            for ph in range(CONV_STRIDE):
                mix_pieces.append(functools.partial(conv_piece, r0, s, ph))
        mix_pieces.append(functools.partial(rows_piece, r0))
    n_out = D_MODEL // MXU_TILE
    n_ff = D_FF // MXU_TILE
    done = [0]

    def fill(slot, result):
        upto = (len(mix_pieces) * (slot + 1)) // n_ff
        conv_after.append(_order_token(result))
        for piece in mix_pieces[done[0]:upto]:
            piece()
        done[0] = upto

    col = lambda n: slice(n * MXU_TILE, (n + 1) * MXU_TILE)
    mixed_prev = mixed_ref[(g + 1) % 2]
    x2_cols = [x1_ref[:, col(n)]
               + jnp.dot(mixed_prev, wout_ref[:, col(n)], preferred_element_type=F32)
               for n in range(n_out)]
    x2 = jnp.concatenate(x2_cols, axis=1)
    h = _rmsnorm(x2, g2_ref[...]).astype(BF16)
    for n in range(n_ff):
        a = jnp.dot(h, w1_ref[:, col(n)], preferred_element_type=F32)
        b = jnp.dot(h, w3_ref[:, col(n)], preferred_element_type=F32)
        zf = a * jax.nn.sigmoid(a) * b
        z_ref[:, col(n)] = zf.astype(BF16)
        fill(n, zf)
    z = z_ref[...]
    x3_cols = [x2_cols[n] + FFN_RES * jnp.dot(z, w2_ref[:, col(n)], preferred_element_type=F32)
               for n in range(n_out)]
    o_ref[...] = _rmsnorm(jnp.concatenate(x3_cols, axis=1), gf_ref[...])


def _mix_ffn2(x1, a, s_out, cw, cb, lng, lnb, cog, wout, g2, w1, w3, w2, gf, nb, seq, ts):
    nt = seq // ts
    n_tiles = nb * nt
    hb = ts // HALO
    cur = lambda g: jnp.minimum(g, n_tiles - 1)
    prev = lambda g: jnp.maximum(g - 1, 0)
    kern = functools.partial(_mix_ffn2_kernel, ts=ts, nt=nt, n_tiles=n_tiles)
    return pl.pallas_call(
        kern,
        grid=(n_tiles + 1,),
        in_specs=[
            pl.BlockSpec((ts, D_MODEL), lambda g: (prev(g), 0)),
            pl.BlockSpec((C_SLAB, ts, LANES), lambda g: (0, cur(g), 0)),
            pl.BlockSpec((C_SLAB, HALO, LANES), lambda g: (0, jnp.maximum(cur(g) * hb - 1, 0), 0)),
            pl.BlockSpec((ts, D_SSM), lambda g: (cur(g), 0)),
            _const_spec((CONV_WIDTH, D_CONV)),
            _const_spec((1, D_CONV)),
            _const_spec((1, D_CONV)),
            _const_spec((1, D_CONV)),
            _const_spec((1, D_CONV)),
            _const_spec((D_MODEL, D_MODEL)),
            _const_spec((1, D_MODEL)),
            _const_spec((D_MODEL, D_FF)),
            _const_spec((D_MODEL, D_FF)),
            _const_spec((D_FF, D_MODEL)),
            _const_spec((1, D_MODEL)),
        ],
        out_specs=pl.BlockSpec((ts, D_MODEL), lambda g: (prev(g), 0)),
        out_shape=jax.ShapeDtypeStruct((nb * seq, D_MODEL), F32),
        scratch_shapes=[
            pltpu.VMEM((C_SLAB, HALO + ts, LANES), F32),
            pltpu.VMEM((C_SLAB, ts, LANES), F32),
            pltpu.VMEM((2, ts, D_MODEL), BF16),
            pltpu.VMEM((ts, D_FF), BF16),
        ],
        compiler_params=pltpu.CompilerParams(
            dimension_semantics=("arbitrary",), vmem_limit_bytes=VMEM_LIMIT),
        name="mix_ffn2",
    )(x1, a, a, s_out, cw, cb, lng, lnb, cog, wout, g2, w1, w3, w2, gf)


def kernel(x, norm_ffn1, ffn1_w1, ffn1_w3, ffn1_w2, norm_mix, w_in, conv_w, conv_b, conv_ln_g,
           conv_ln_b, conv_out_g, ssm_A_re, ssm_A_im, ssm_log_dt, ssm_B_re, ssm_B_im, ssm_C_re,
           ssm_C_im, ssm_D, ssm_glu_w, ssm_glu_b, ssm_out_g, w_out, norm_ffn2, ffn2_w1, ffn2_w3,
           ffn2_w2, norm_final):
    nb, seq, _ = x.shape
    depth = norm_ffn1.shape[0]
    assert depth == 1, "the final norm is fused into the last FFN; only depth 1 is laid out"
    tm = 512
    tb = 512
    tr = 2048
    ts = 512
    assert seq % tb == 0 and seq % ts == 0 and (nb * seq) % tm == 0 and (nb * seq) % tr == 0
    assert tb % (CHUNK * 16) == 0 and ts % HALO == 0 and ts % MIX_ROWS == 0

    row = lambda v: v.reshape(1, -1).astype(F32)
    x2d = x.reshape(nb * seq, D_MODEL)
    for l in range(depth):
        x1, a, u4 = _ffn1_proj(
            x2d, row(norm_ffn1[l]), ffn1_w1[l].astype(BF16), ffn1_w3[l].astype(BF16),
            ffn1_w2[l].astype(BF16), row(norm_mix[l]), w_in[l].astype(BF16), tm)
        wc, vc, al = _s5_compact(
            ssm_A_re[l], ssm_A_im[l], ssm_log_dt[l], ssm_B_re[l], ssm_B_im[l],
            ssm_C_re[l], ssm_C_im[l])
        wmat, tv = _s5_embed(wc, vc, ssm_D[l].reshape(N_SLAB, 1, LANES))
        y4 = _s5_scan(u4.reshape(N_SLAB, nb, seq, LANES), wmat, tv, al, nb, seq, tb)
        s_out = _s5_out(y4.reshape(N_SLAB, nb * seq, LANES), ssm_glu_w[l].astype(BF16),
                        row(ssm_glu_b[l]), row(ssm_out_g[l]), tr)
        x2d = _mix_ffn2(
            x1, a, s_out, conv_w[l], row(conv_b[l]), row(conv_ln_g[l]), row(conv_ln_b[l]),
            row(conv_out_g[l]), w_out[l].astype(BF16), row(norm_ffn2[l]),
            ffn2_w1[l].astype(BF16), ffn2_w3[l].astype(BF16), ffn2_w2[l].astype(BF16),
            row(norm_final), nb, seq, ts)
    return x2d.reshape(nb, seq, D_MODEL)
```

```python
import functools
import math

import jax
import jax.numpy as jnp
from jax import lax
from jax.experimental import pallas as pl
from jax.experimental.pallas import tpu as pltpu

D_MODEL = 1024
D_CONV = 512
D_SSM = 512
CONV_WIDTH = 31
SSM_GROUP = 16
SSM_GROUPS = 32
SSM_STATE = 64
D_FF = 2816
D_IN = 2 * D_CONV + D_SSM
FFN_RES = 0.5
EPS = 1e-6
GELU_C0 = math.sqrt(2.0 / math.pi)
GELU_C1 = 0.044715 * GELU_C0

LANES = 128
MXU_TILE = 256
SLAB_GROUPS = LANES // SSM_GROUP
N_SLAB = D_SSM // LANES
C_SLAB = D_CONV // LANES
SLAB_STATE = SLAB_GROUPS * SSM_STATE
CHUNK = 8
CHUNK_K = CHUNK * LANES
HALO = 32

VMEM_LIMIT = 56 * 1024 * 1024

F32 = jnp.float32
BF16 = jnp.bfloat16


def _sigmoid(x):
    return 0.5 * jnp.tanh(0.5 * x) + 0.5


def _rmsnorm(x, g):
    ms = jnp.mean(x * x, axis=-1, keepdims=True)
    return x * lax.rsqrt(ms + EPS) * g


def _swiglu_residual(x, g, w1_ref, w3_ref, w2_ref):
    h = _rmsnorm(x, g).astype(BF16)
    a = jnp.dot(h, w1_ref[...], preferred_element_type=F32)
    b = jnp.dot(h, w3_ref[...], preferred_element_type=F32)
    ha = 0.5 * a
    z = ((ha * jnp.tanh(ha) + ha) * b).astype(BF16)
    o = jnp.dot(z, w2_ref[...], preferred_element_type=F32)
    return x + FFN_RES * o


def _const_spec(shape):
    nd = len(shape)
    return pl.BlockSpec(shape, lambda *_: (0,) * nd, pipeline_mode=pl.Buffered(1))


def _ffn1_proj_kernel(x_ref, g1_ref, w1_ref, w3_ref, w2_ref, gm_ref, win_ref,
                      x1_ref, a_ref, u_ref):
    x1 = _swiglu_residual(x_ref[...], g1_ref[...], w1_ref, w3_ref, w2_ref)
    x1_ref[...] = x1
    h2 = _rmsnorm(x1, gm_ref[...]).astype(BF16)
    proj = jnp.dot(h2, win_ref[...], preferred_element_type=F32)
    for s in range(C_SLAB):
        lo = s * LANES
        a_val = proj[:, lo:lo + LANES]
        a_gate = proj[:, D_CONV + lo:D_CONV + lo + LANES]
        a_ref[s] = a_val * _sigmoid(a_gate)
    for s in range(N_SLAB):
        lo = 2 * D_CONV + s * LANES
        u_ref[s] = proj[:, lo:lo + LANES]


def _ffn1_proj(x2d, g1, w1, w3, w2, gm, win, tm):
    t = x2d.shape[0]
    row = lambda i: (i, 0)
    return pl.pallas_call(
        _ffn1_proj_kernel,
        grid=(t // tm,),
        in_specs=[
            pl.BlockSpec((tm, D_MODEL), row),
            _const_spec((1, D_MODEL)),
            _const_spec((D_MODEL, D_FF)),
            _const_spec((D_MODEL, D_FF)),
            _const_spec((D_FF, D_MODEL)),
            _const_spec((1, D_MODEL)),
            _const_spec((D_MODEL, D_IN)),
        ],
        out_specs=[
            pl.BlockSpec((tm, D_MODEL), row),
            pl.BlockSpec((C_SLAB, tm, LANES), lambda i: (0, i, 0)),
            pl.BlockSpec((N_SLAB, tm, LANES), lambda i: (0, i, 0)),
        ],
        out_shape=[
            jax.ShapeDtypeStruct((t, D_MODEL), F32),
            jax.ShapeDtypeStruct((C_SLAB, t, LANES), F32),
            jax.ShapeDtypeStruct((N_SLAB, t, LANES), F32),
        ],
        compiler_params=pltpu.CompilerParams(
            dimension_semantics=("parallel",), vmem_limit_bytes=VMEM_LIMIT),
        name="ffn1_proj",
    )(x2d, g1, w1, w3, w2, gm, win)


def _s5_compact(a_re, a_im, log_dt, b_re, b_im, c_re, c_im):
    L = CHUNK
    dt = jnp.exp(log_dt)[:, None]
    zr, zi = a_re * dt, a_im * dt
    mag = jnp.exp(zr)
    abar_r, abar_i = mag * jnp.cos(zi), mag * jnp.sin(zi)
    den = a_re * a_re + a_im * a_im
    nr = abar_r - 1.0
    coef_r = (nr * a_re + abar_i * a_im) / den
    coef_i = (abar_i * a_re - nr * a_im) / den
    bb_r = coef_r[..., None] * b_re - coef_i[..., None] * b_im
    bb_i = coef_r[..., None] * b_im + coef_i[..., None] * b_re
    n = jnp.arange(L + 1, dtype=F32)[:, None, None]
    pmag = jnp.exp(n * zr)
    pw_r, pw_i = pmag * jnp.cos(n * zi), pmag * jnp.sin(n * zi)
    cp_r = c_re[None] * pw_r[:, :, None, :] - c_im[None] * pw_i[:, :, None, :]
    cp_i = c_re[None] * pw_i[:, :, None, :] + c_im[None] * pw_r[:, :, None, :]
    vc = jnp.concatenate([cp_r, -cp_i], axis=-1)
    nrev = (L - 1) - n[:L]
    rmag = jnp.exp(nrev * zr)
    rv_r = (rmag * jnp.cos(nrev * zi))[:, :, None, :]
    rv_i = (rmag * jnp.sin(nrev * zi))[:, :, None, :]
    bt_r, bt_i = bb_r.transpose(0, 2, 1)[None], bb_i.transpose(0, 2, 1)[None]
    wc = jnp.concatenate([rv_r * bt_r - rv_i * bt_i, rv_r * bt_i + rv_i * bt_r], axis=-1)

    def per_slab(v):
        v = v.reshape(v.shape[0], N_SLAB, LANES, 2 * SSM_STATE)
        return v.transpose(1, 0, 2, 3).reshape(N_SLAB, -1, 2 * SSM_STATE)

    al = jnp.stack([pw_r[L].reshape(N_SLAB, SLAB_STATE), pw_i[L].reshape(N_SLAB, SLAB_STATE)], axis=1)
    return per_slab(wc), per_slab(vc), al


def _s5_embed_kernel(wc_ref, vc_ref, d_ref, w_ref, tv_ref):
    L = CHUNK

    def embed(c):
        rows = c.shape[0]
        row_group = (lax.broadcasted_iota(jnp.int32, (rows, LANES), 0) // SSM_GROUP) % SLAB_GROUPS
        lane_half = lax.broadcasted_iota(jnp.int32, (rows, LANES), 1) // SSM_STATE
        pieces = []
        for part in range(2):
            x = c[:, part * SSM_STATE:(part + 1) * SSM_STATE]
            xx = jnp.concatenate([x] * (LANES // SSM_STATE), axis=1)
            for k in range(SLAB_STATE // LANES):
                own = row_group == (LANES // SSM_STATE) * k + lane_half
                pieces.append(jnp.where(own, xx, 0.0))
        return jnp.concatenate(pieces, axis=1)

    w = embed(wc_ref[...])
    vt = embed(vc_ref[...])
    w_ref[...] = w.astype(BF16)

    w0 = w[(L - 1) * LANES:, :]
    nt = (((1,), (1,)), ((), ()))
    ri = lax.broadcasted_iota(jnp.int32, (LANES, LANES), 0)
    ci = lax.broadcasted_iota(jnp.int32, (LANES, LANES), 1)
    zero = jnp.zeros((LANES, LANES), BF16)
    for n in range(L):
        k_n = lax.dot_general(w0, vt[n * LANES:(n + 1) * LANES, :], nt,
                              precision=lax.Precision.HIGHEST, preferred_element_type=F32)
        if n == 0:
            k_n = k_n + jnp.where(ri == ci, d_ref[...], 0.0)
        k_n = k_n.astype(BF16)
        for s in range(L - n):
            t = s + n
            tv_ref[s * LANES:(s + 1) * LANES, t * LANES:(t + 1) * LANES] = k_n
    for s in range(1, L):
        for t in range(s):
            tv_ref[s * LANES:(s + 1) * LANES, t * LANES:(t + 1) * LANES] = zero
    tv_ref[L * LANES:, :] = vt[LANES:, :].T.astype(BF16)


def _s5_embed(wc, vc, d4):
    n_state = 2 * SLAB_STATE
    per_slab = lambda s: (s, 0, 0)
    return pl.pallas_call(
        _s5_embed_kernel,
        grid=(N_SLAB,),
        in_specs=[
            pl.BlockSpec((None, CHUNK_K, 2 * SSM_STATE), per_slab),
            pl.BlockSpec((None, CHUNK_K + LANES, 2 * SSM_STATE), per_slab),
            pl.BlockSpec((None, 1, LANES), per_slab),
        ],
        out_specs=[
            pl.BlockSpec((None, CHUNK_K, n_state), per_slab),
            pl.BlockSpec((None, CHUNK_K + n_state, CHUNK_K), per_slab),
        ],
        out_shape=[
            jax.ShapeDtypeStruct((N_SLAB, CHUNK_K, n_state), BF16),
            jax.ShapeDtypeStruct((N_SLAB, CHUNK_K + n_state, CHUNK_K), BF16),
        ],
        compiler_params=pltpu.CompilerParams(
            dimension_semantics=("parallel",), vmem_limit_bytes=VMEM_LIMIT),
        name="s5_embed",
    )(wc, vc, d4)


def _s5_kernel(u_ref, w_ref, tv_ref, al_ref, y_ref, lhs_ref, xpb_ref, z_ref, xp_ref, st_ref,
               *, nb, mc):
    n_state_slab = 2 * SLAB_STATE // LANES
    half = n_state_slab // 2
    pitch = mc + 8

    @pl.when(pl.program_id(1) == 0)
    def _():
        st_ref[...] = jnp.zeros_like(st_ref)

    for b in range(nb):
        for s in range(CHUNK):
            piece = u_ref[b, pl.ds(s, mc, stride=CHUNK), :]
            lhs_ref[b * mc:(b + 1) * mc, s * LANES:(s + 1) * LANES] = piece.astype(BF16)

    z = jnp.dot(lhs_ref[...], w_ref[...], preferred_element_type=F32)
    for b in range(nb):
        for k in range(n_state_slab):
            z_ref[k, b * pitch:b * pitch + mc, :] = z[b * mc:(b + 1) * mc, k * LANES:(k + 1) * LANES]

    y_cols = []
    for n in range(CHUNK_K // MXU_TILE):
        kk = (n + 1) * MXU_TILE
        y_cols.append(jnp.dot(lhs_ref[:, :kk], tv_ref[:kk, n * MXU_TILE:(n + 1) * MXU_TILE],
                              preferred_element_type=F32))

    ar = al_ref[0:1, :]
    ai = al_ref[1:2, :]
    xr = st_ref[:, :SLAB_STATE]
    xi = st_ref[:, SLAB_STATE:]
    for m in range(mc):
        rows = pl.ds(m, nb, stride=pitch)
        for k in range(half):
            xp_ref[k, rows, :] = xr[:, k * LANES:(k + 1) * LANES]
            xp_ref[half + k, rows, :] = xi[:, k * LANES:(k + 1) * LANES]
        zr = jnp.concatenate([z_ref[k, rows, :] for k in range(half)], axis=1)
        zi = jnp.concatenate([z_ref[half + k, rows, :] for k in range(half)], axis=1)
        xr, xi = ar * xr - ai * xi + zr, ar * xi + ai * xr + zi
    st_ref[:, :SLAB_STATE] = xr
    st_ref[:, SLAB_STATE:] = xi

    for b in range(nb):
        for k in range(n_state_slab):
            xpb_ref[b * mc:(b + 1) * mc, k * LANES:(k + 1) * LANES] = (
                xp_ref[k, b * pitch:b * pitch + mc, :].astype(BF16))

    y = (jnp.concatenate(y_cols, axis=1)
         + jnp.dot(xpb_ref[...], tv_ref[CHUNK_K:, :], preferred_element_type=F32))
    for b in range(nb):
        for t in range(CHUNK):
            y_ref[b, pl.ds(t, mc, stride=CHUNK), :] = y[b * mc:(b + 1) * mc, t * LANES:(t + 1) * LANES]


def _s5_scan(u4, wmat, tv, al, nb, seq, tb):
    mc = tb // CHUNK
    rows = nb * mc
    n_state = 2 * SLAB_STATE
    kern = functools.partial(_s5_kernel, nb=nb, mc=mc)
    blk = lambda s, j: (s, 0, j, 0)
    per_slab = lambda s, j: (s, 0, 0)
    return pl.pallas_call(
        kern,
        grid=(N_SLAB, seq // tb),
        in_specs=[
            pl.BlockSpec((None, nb, tb, LANES), blk),
            pl.BlockSpec((None, CHUNK_K, n_state), per_slab),
            pl.BlockSpec((None, CHUNK_K + n_state, CHUNK_K), per_slab),
            pl.BlockSpec((None, 2, SLAB_STATE), per_slab),
        ],
        out_specs=pl.BlockSpec((None, nb, tb, LANES), blk),
        out_shape=jax.ShapeDtypeStruct((N_SLAB, nb, seq, LANES), F32),
        scratch_shapes=[
            pltpu.VMEM((rows, CHUNK_K), BF16),
            pltpu.VMEM((rows, n_state), BF16),
            pltpu.VMEM((n_state // LANES, nb * (mc + 8), LANES), F32),
            pltpu.VMEM((n_state // LANES, nb * (mc + 8), LANES), F32),
            pltpu.VMEM((nb, n_state), F32),
        ],
        compiler_params=pltpu.CompilerParams(
            dimension_semantics=("arbitrary", "arbitrary"), vmem_limit_bytes=VMEM_LIMIT),
        name="s5_scan",
    )(u4, wmat, tv, al)


def _s5_out_kernel(y4_ref, gluw_ref, glub_ref, sog_ref, o_ref):
    y = jnp.concatenate([y4_ref[s] for s in range(N_SLAB)], axis=1)
    q = 0.25 * y
    t = jnp.tanh(y * (GELU_C1 * (y * y) + GELU_C0))
    hy = q * t + q
    half_gate = (jnp.dot(hy.astype(BF16), gluw_ref[...], preferred_element_type=F32)
                 + 0.5 * glub_ref[...])
    o_ref[...] = _rmsnorm(hy * jnp.tanh(half_gate) + hy, sog_ref[...]).astype(BF16)


def _s5_out(y4, gluw, glub, sog, tr):
    t = y4.shape[1]
    return pl.pallas_call(
        _s5_out_kernel,
        grid=(t // tr,),
        in_specs=[
            pl.BlockSpec((N_SLAB, tr, LANES), lambda i: (0, i, 0)),
            _const_spec((D_SSM, D_SSM)),
            _const_spec((1, D_SSM)),
            _const_spec((1, D_SSM)),
        ],
        out_specs=pl.BlockSpec((tr, D_SSM), lambda i: (i, 0)),
        out_shape=jax.ShapeDtypeStruct((t, D_SSM), BF16),
        compiler_params=pltpu.CompilerParams(
            dimension_semantics=("parallel",), vmem_limit_bytes=VMEM_LIMIT),
        name="s5_out",
    )(y4, gluw, glub, sog)


CONV_ROWS = 32
CONV_STRIDE = 4
MIX_ROWS = CONV_ROWS * CONV_STRIDE


def _order_token(v):
    t = v[:, :LANES]
    for lo in range(LANES, v.shape[1], LANES):
        t = jnp.maximum(t, v[:, lo:lo + LANES])
    return jnp.max(t, axis=0, keepdims=True)


def _wait_for(x, tokens):
    for t in tokens:
        bits = pltpu.bitcast(t, jnp.uint32)
        bits = lax.shift_right_logical(lax.shift_right_logical(bits, jnp.uint32(16)), jnp.uint32(16))
        x = pltpu.bitcast(pltpu.bitcast(x, jnp.uint32) + bits, F32)
    return x


def _conv_chain(r0, s, ph, after, cw_ref, cb_ref, aext_ref, conv_ref):
    base = HALO - (CONV_WIDTH - 1)
    lanes = slice(s * LANES, (s + 1) * LANES)
    acc = _wait_for(jnp.broadcast_to(cb_ref[:, lanes], (CONV_ROWS, LANES)), after)
    for k in range(CONV_WIDTH):
        rows = pl.ds(r0 + ph + base + k, CONV_ROWS, stride=CONV_STRIDE)
        acc = acc + cw_ref[k:k + 1, lanes] * aext_ref[s, rows, :]
    conv_ref[s, pl.ds(r0 + ph, CONV_ROWS, stride=CONV_STRIDE), :] = acc
    return _order_token(acc)


def _mix_rows(r0, conv_ref, s_ref, lng_ref, lnb_ref, cog_ref, dst_ref):
    rows = slice(r0, r0 + MIX_ROWS)
    c = jnp.concatenate([conv_ref[s, rows, :] for s in range(C_SLAB)], axis=1)
    mu = jnp.mean(c, axis=-1, keepdims=True)
    xc = c - mu
    var = jnp.mean(xc * xc, axis=-1, keepdims=True)
    c = xc * lax.rsqrt(var + EPS) * lng_ref[...] + lnb_ref[...]
    c = c * jax.nn.sigmoid(c)
    a_out = _rmsnorm(c, cog_ref[...])
    dst_ref[rows, :] = jnp.concatenate([a_out.astype(BF16), s_ref[rows, :]], axis=1)


def _mix_ffn2_kernel(x1_ref, a_ref, halo_ref, s_ref, cw_ref, cb_ref, lng_ref, lnb_ref, cog_ref,
                     wout_ref, g2_ref, w1_ref, w3_ref, w2_ref, gf_ref,
                     o_ref, aext_ref, conv_ref, mixed_ref, z_ref, *, ts, nt, n_tiles):
    g = pl.program_id(0)

    @pl.when(g == 0)
    def _():
        mixed_ref[...] = jnp.zeros_like(mixed_ref)

    first = jnp.minimum(g, n_tiles - 1) % nt == 0
    aext_ref[:, 0:HALO, :] = jnp.where(first, 0.0, halo_ref[...])
    aext_ref[:, HALO:, :] = a_ref[...]

    conv_after = []

    def conv_piece(r0, s, ph):
        conv_after[:] = [_conv_chain(r0, s, ph, conv_after, cw_ref, cb_ref, aext_ref, conv_ref)]

    def rows_piece(r0):
        _mix_rows(r0, conv_ref, s_ref, lng_ref, lnb_ref, cog_ref, mixed_ref.at[g % 2])

    mix_pieces = []
    for r0 in range(0, ts, MIX_ROWS):
        for s in range(C_SLAB):
            for ph in range(CONV_STRIDE):
                mix_pieces.append(functools.partial(conv_piece, r0, s, ph))
        mix_pieces.append(functools.partial(rows_piece, r0))
    n_ff = D_FF // MXU_TILE
    done = [0]

    def fill(slot, result):
        upto = (len(mix_pieces) * (slot + 1)) // n_ff
        conv_after.append(_order_token(result))
        for piece in mix_pieces[done[0]:upto]:
            piece()
        done[0] = upto

    col = lambda n: slice(n * MXU_TILE, (n + 1) * MXU_TILE)
    mixed_prev = mixed_ref[(g + 1) % 2]
    x2 = x1_ref[...] + jnp.dot(mixed_prev, wout_ref[...], preferred_element_type=F32)
    h = _rmsnorm(x2, g2_ref[...]).astype(BF16)
    for n in range(n_ff):
        a = jnp.dot(h, w1_ref[:, col(n)], preferred_element_type=F32)
        b = jnp.dot(h, w3_ref[:, col(n)], preferred_element_type=F32)
        zf = a * jax.nn.sigmoid(a) * b
        z_ref[:, col(n)] = zf.astype(BF16)
        fill(n, zf)
    z = z_ref[...]
    x3 = x2 + FFN_RES * jnp.dot(z, w2_ref[...], preferred_element_type=F32)
    o_ref[...] = _rmsnorm(x3, gf_ref[...])


def _mix_ffn2(x1, a, s_out, cw, cb, lng, lnb, cog, wout, g2, w1, w3, w2, gf, nb, seq, ts):
    nt = seq // ts
    n_tiles = nb * nt
    hb = ts // HALO
    cur = lambda g: jnp.minimum(g, n_tiles - 1)
    prev = lambda g: jnp.maximum(g - 1, 0)
    kern = functools.partial(_mix_ffn2_kernel, ts=ts, nt=nt, n_tiles=n_tiles)
    return pl.pallas_call(
        kern,
        grid=(n_tiles + 1,),
        in_specs=[
            pl.BlockSpec((ts, D_MODEL), lambda g: (prev(g), 0)),
            pl.BlockSpec((C_SLAB, ts, LANES), lambda g: (0, cur(g), 0)),
            pl.BlockSpec((C_SLAB, HALO, LANES), lambda g: (0, jnp.maximum(cur(g) * hb - 1, 0), 0)),
            pl.BlockSpec((ts, D_SSM), lambda g: (cur(g), 0)),
            _const_spec((CONV_WIDTH, D_CONV)),
            _const_spec((1, D_CONV)),
            _const_spec((1, D_CONV)),
            _const_spec((1, D_CONV)),
            _const_spec((1, D_CONV)),
            _const_spec((D_MODEL, D_MODEL)),
            _const_spec((1, D_MODEL)),
            _const_spec((D_MODEL, D_FF)),
            _const_spec((D_MODEL, D_FF)),
            _const_spec((D_FF, D_MODEL)),
            _const_spec((1, D_MODEL)),
        ],
        out_specs=pl.BlockSpec((ts, D_MODEL), lambda g: (prev(g), 0)),
        out_shape=jax.ShapeDtypeStruct((nb * seq, D_MODEL), F32),
        scratch_shapes=[
            pltpu.VMEM((C_SLAB, HALO + ts, LANES), F32),
            pltpu.VMEM((C_SLAB, ts, LANES), F32),
            pltpu.VMEM((2, ts, D_MODEL), BF16),
            pltpu.VMEM((ts, D_FF), BF16),
        ],
        compiler_params=pltpu.CompilerParams(
            dimension_semantics=("arbitrary",), vmem_limit_bytes=VMEM_LIMIT),
        name="mix_ffn2",
    )(x1, a, a, s_out, cw, cb, lng, lnb, cog, wout, g2, w1, w3, w2, gf)


def kernel(x, norm_ffn1, ffn1_w1, ffn1_w3, ffn1_w2, norm_mix, w_in, conv_w, conv_b, conv_ln_g,
           conv_ln_b, conv_out_g, ssm_A_re, ssm_A_im, ssm_log_dt, ssm_B_re, ssm_B_im, ssm_C_re,
           ssm_C_im, ssm_D, ssm_glu_w, ssm_glu_b, ssm_out_g, w_out, norm_ffn2, ffn2_w1, ffn2_w3,
           ffn2_w2, norm_final):
    nb, seq, _ = x.shape
    depth = norm_ffn1.shape[0]
    assert depth == 1, "the final norm is fused into the last FFN; only depth 1 is laid out"
    tm = 512
    tb = 512
    tr = 2048
    ts = 512
    assert seq % tb == 0 and seq % ts == 0 and (nb * seq) % tm == 0 and (nb * seq) % tr == 0
    assert tb % (CHUNK * 16) == 0 and ts % HALO == 0 and ts % MIX_ROWS == 0

    row = lambda v: v.reshape(1, -1).astype(F32)
    x2d = x.reshape(nb * seq, D_MODEL)
    for l in range(depth):
        x1, a, u4 = _ffn1_proj(
            x2d, row(norm_ffn1[l]), ffn1_w1[l].astype(BF16), ffn1_w3[l].astype(BF16),
            ffn1_w2[l].astype(BF16), row(norm_mix[l]), w_in[l].astype(BF16), tm)
        wc, vc, al = _s5_compact(
            ssm_A_re[l], ssm_A_im[l], ssm_log_dt[l], ssm_B_re[l], ssm_B_im[l],
            ssm_C_re[l], ssm_C_im[l])
        wmat, tv = _s5_embed(wc, vc, ssm_D[l].reshape(N_SLAB, 1, LANES))
        y4 = _s5_scan(u4.reshape(N_SLAB, nb, seq, LANES), wmat, tv, al, nb, seq, tb)
        s_out = _s5_out(y4.reshape(N_SLAB, nb * seq, LANES), ssm_glu_w[l].astype(BF16),
                        row(ssm_glu_b[l]), row(ssm_out_g[l]), tr)
        x2d = _mix_ffn2(
            x1, a, s_out, conv_w[l], row(conv_b[l]), row(conv_ln_g[l]), row(conv_ln_b[l]),
            row(conv_out_g[l]), w_out[l].astype(BF16), row(norm_ffn2[l]),
            ffn2_w1[l].astype(BF16), ffn2_w3[l].astype(BF16), ffn2_w2[l].astype(BF16),
            row(norm_final), nb, seq, ts)
    return x2d.reshape(nb, seq, D_MODEL)
```

```python
import functools
import math

import jax
import jax.numpy as jnp
from jax import lax
from jax.experimental import pallas as pl
from jax.experimental.pallas import tpu as pltpu

D_MODEL = 1024
D_CONV = 512
D_SSM = 512
CONV_WIDTH = 31
SSM_GROUP = 16
SSM_GROUPS = 32
SSM_STATE = 64
D_FF = 2816
D_IN = 2 * D_CONV + D_SSM
FFN_RES = 0.5
EPS = 1e-6
GELU_C0 = math.sqrt(2.0 / math.pi)
GELU_C1 = 0.044715 * GELU_C0

LANES = 128
MXU_TILE = 256
SLAB_GROUPS = LANES // SSM_GROUP
N_SLAB = D_SSM // LANES
C_SLAB = D_CONV // LANES
SLAB_STATE = SLAB_GROUPS * SSM_STATE
CHUNK = 4
CHUNK_K = CHUNK * LANES
HALO = 32

VMEM_LIMIT = 56 * 1024 * 1024

F32 = jnp.float32
BF16 = jnp.bfloat16


def _sigmoid(x):
    return 0.5 * jnp.tanh(0.5 * x) + 0.5


def _rmsnorm(x, g):
    ms = jnp.mean(x * x, axis=-1, keepdims=True)
    return x * lax.rsqrt(ms + EPS) * g


def _swiglu_residual(x, g, w1_ref, w3_ref, w2_ref):
    h = _rmsnorm(x, g).astype(BF16)
    a = jnp.dot(h, w1_ref[...], preferred_element_type=F32)
    b = jnp.dot(h, w3_ref[...], preferred_element_type=F32)
    ha = 0.5 * a
    z = ((ha * jnp.tanh(ha) + ha) * b).astype(BF16)
    o = jnp.dot(z, w2_ref[...], preferred_element_type=F32)
    return x + FFN_RES * o


def _const_spec(shape):
    nd = len(shape)
    return pl.BlockSpec(shape, lambda *_: (0,) * nd, pipeline_mode=pl.Buffered(1))


def _ffn1_proj_kernel(x_ref, g1_ref, w1_ref, w3_ref, w2_ref, gm_ref, win_ref,
                      x1_ref, a_ref, u_ref):
    x1 = _swiglu_residual(x_ref[...], g1_ref[...], w1_ref, w3_ref, w2_ref)
    x1_ref[...] = x1
    h2 = _rmsnorm(x1, gm_ref[...]).astype(BF16)
    proj = jnp.dot(h2, win_ref[...], preferred_element_type=F32)
    for s in range(C_SLAB):
        lo = s * LANES
        a_val = proj[:, lo:lo + LANES]
        a_gate = proj[:, D_CONV + lo:D_CONV + lo + LANES]
        a_ref[s] = a_val * _sigmoid(a_gate)
    for s in range(N_SLAB):
        lo = 2 * D_CONV + s * LANES
        u_ref[s] = proj[:, lo:lo + LANES]


def _ffn1_proj(x2d, g1, w1, w3, w2, gm, win, tm):
    t = x2d.shape[0]
    row = lambda i: (i, 0)
    return pl.pallas_call(
        _ffn1_proj_kernel,
        grid=(t // tm,),
        in_specs=[
            pl.BlockSpec((tm, D_MODEL), row),
            _const_spec((1, D_MODEL)),
            _const_spec((D_MODEL, D_FF)),
            _const_spec((D_MODEL, D_FF)),
            _const_spec((D_FF, D_MODEL)),
            _const_spec((1, D_MODEL)),
            _const_spec((D_MODEL, D_IN)),
        ],
        out_specs=[
            pl.BlockSpec((tm, D_MODEL), row),
            pl.BlockSpec((C_SLAB, tm, LANES), lambda i: (0, i, 0)),
            pl.BlockSpec((N_SLAB, tm, LANES), lambda i: (0, i, 0)),
        ],
        out_shape=[
            jax.ShapeDtypeStruct((t, D_MODEL), F32),
            jax.ShapeDtypeStruct((C_SLAB, t, LANES), F32),
            jax.ShapeDtypeStruct((N_SLAB, t, LANES), F32),
        ],
        compiler_params=pltpu.CompilerParams(
            dimension_semantics=("parallel",), vmem_limit_bytes=VMEM_LIMIT),
        name="ffn1_proj",
    )(x2d, g1, w1, w3, w2, gm, win)


def _s5_compact(a_re, a_im, log_dt, b_re, b_im, c_re, c_im):
    L = CHUNK
    dt = jnp.exp(log_dt)[:, None]
    zr, zi = a_re * dt, a_im * dt
    mag = jnp.exp(zr)
    abar_r, abar_i = mag * jnp.cos(zi), mag * jnp.sin(zi)
    den = a_re * a_re + a_im * a_im
    nr = abar_r - 1.0
    coef_r = (nr * a_re + abar_i * a_im) / den
    coef_i = (abar_i * a_re - nr * a_im) / den
    bb_r = coef_r[..., None] * b_re - coef_i[..., None] * b_im
    bb_i = coef_r[..., None] * b_im + coef_i[..., None] * b_re
    n = jnp.arange(L + 1, dtype=F32)[:, None, None]
    pmag = jnp.exp(n * zr)
    pw_r, pw_i = pmag * jnp.cos(n * zi), pmag * jnp.sin(n * zi)
    cp_r = c_re[None] * pw_r[:, :, None, :] - c_im[None] * pw_i[:, :, None, :]
    cp_i = c_re[None] * pw_i[:, :, None, :] + c_im[None] * pw_r[:, :, None, :]
    vc = jnp.concatenate([cp_r, -cp_i], axis=-1)
    nrev = (L - 1) - n[:L]
    rmag = jnp.exp(nrev * zr)
    rv_r = (rmag * jnp.cos(nrev * zi))[:, :, None, :]
    rv_i = (rmag * jnp.sin(nrev * zi))[:, :, None, :]
    bt_r, bt_i = bb_r.transpose(0, 2, 1)[None], bb_i.transpose(0, 2, 1)[None]
    wc = jnp.concatenate([rv_r * bt_r - rv_i * bt_i, rv_r * bt_i + rv_i * bt_r], axis=-1)

    def per_slab(v):
        v = v.reshape(v.shape[0], N_SLAB, LANES, 2 * SSM_STATE)
        return v.transpose(1, 0, 2, 3).reshape(N_SLAB, -1, 2 * SSM_STATE)

    al = jnp.stack([pw_r[L].reshape(N_SLAB, SLAB_STATE), pw_i[L].reshape(N_SLAB, SLAB_STATE)], axis=1)
    return per_slab(wc), per_slab(vc), al


def _s5_embed_kernel(wc_ref, vc_ref, d_ref, w_ref, tv_ref):
    L = CHUNK

    def embed(c):
        rows = c.shape[0]
        row_group = (lax.broadcasted_iota(jnp.int32, (rows, LANES), 0) // SSM_GROUP) % SLAB_GROUPS
        lane_half = lax.broadcasted_iota(jnp.int32, (rows, LANES), 1) // SSM_STATE
        pieces = []
        for part in range(2):
            x = c[:, part * SSM_STATE:(part + 1) * SSM_STATE]
            xx = jnp.concatenate([x] * (LANES // SSM_STATE), axis=1)
            for k in range(SLAB_STATE // LANES):
                own = row_group == (LANES // SSM_STATE) * k + lane_half
                pieces.append(jnp.where(own, xx, 0.0))
        return jnp.concatenate(pieces, axis=1)

    w = embed(wc_ref[...])
    vt = embed(vc_ref[...])
    w_ref[...] = w.astype(BF16)

    w0 = w[(L - 1) * LANES:, :]
    nt = (((1,), (1,)), ((), ()))
    ri = lax.broadcasted_iota(jnp.int32, (LANES, LANES), 0)
    ci = lax.broadcasted_iota(jnp.int32, (LANES, LANES), 1)
    zero = jnp.zeros((LANES, LANES), BF16)
    for n in range(L):
        k_n = lax.dot_general(w0, vt[n * LANES:(n + 1) * LANES, :], nt,
                              precision=lax.Precision.HIGHEST, preferred_element_type=F32)
        if n == 0:
            k_n = k_n + jnp.where(ri == ci, d_ref[...], 0.0)
        k_n = k_n.astype(BF16)
        for s in range(L - n):
            t = s + n
            tv_ref[s * LANES:(s + 1) * LANES, t * LANES:(t + 1) * LANES] = k_n
    for s in range(1, L):
        for t in range(s):
            tv_ref[s * LANES:(s + 1) * LANES, t * LANES:(t + 1) * LANES] = zero
    tv_ref[L * LANES:, :] = vt[LANES:, :].T.astype(BF16)


def _s5_embed(wc, vc, d4):
    n_state = 2 * SLAB_STATE
    per_slab = lambda s: (s, 0, 0)
    return pl.pallas_call(
        _s5_embed_kernel,
        grid=(N_SLAB,),
        in_specs=[
            pl.BlockSpec((None, CHUNK_K, 2 * SSM_STATE), per_slab),
            pl.BlockSpec((None, CHUNK_K + LANES, 2 * SSM_STATE), per_slab),
            pl.BlockSpec((None, 1, LANES), per_slab),
        ],
        out_specs=[
            pl.BlockSpec((None, CHUNK_K, n_state), per_slab),
            pl.BlockSpec((None, CHUNK_K + n_state, CHUNK_K), per_slab),
        ],
        out_shape=[
            jax.ShapeDtypeStruct((N_SLAB, CHUNK_K, n_state), BF16),
            jax.ShapeDtypeStruct((N_SLAB, CHUNK_K + n_state, CHUNK_K), BF16),
        ],
        compiler_params=pltpu.CompilerParams(
            dimension_semantics=("parallel",), vmem_limit_bytes=VMEM_LIMIT),
        name="s5_embed",
    )(wc, vc, d4)


def _s5_kernel(u_ref, w_ref, tv_ref, al_ref, y_ref, lhs_ref, xpb_ref, z_ref, xp_ref, st_ref,
               *, nb, mc):
    n_state_slab = 2 * SLAB_STATE // LANES
    half = n_state_slab // 2
    pitch = mc + 4

    @pl.when(pl.program_id(1) == 0)
    def _():
        st_ref[...] = jnp.zeros_like(st_ref)

    for b in range(nb):
        for s in range(CHUNK):
            piece = u_ref[b, pl.ds(s, mc, stride=CHUNK), :]
            lhs_ref[b * mc:(b + 1) * mc, s * LANES:(s + 1) * LANES] = piece.astype(BF16)

    z = jnp.dot(lhs_ref[...], w_ref[...], preferred_element_type=F32)
    for b in range(nb):
        for k in range(n_state_slab):
            z_ref[k, b * pitch:b * pitch + mc, :] = z[b * mc:(b + 1) * mc, k * LANES:(k + 1) * LANES]

    y_cols = []
    for n in range(CHUNK_K // MXU_TILE):
        kk = (n + 1) * MXU_TILE
        y_cols.append(jnp.dot(lhs_ref[:, :kk], tv_ref[:kk, n * MXU_TILE:(n + 1) * MXU_TILE],
                              preferred_element_type=F32))

    ar = al_ref[0:1, :]
    ai = al_ref[1:2, :]
    xr = st_ref[:, :SLAB_STATE]
    xi = st_ref[:, SLAB_STATE:]
    for m in range(mc):
        rows = pl.ds(m, nb, stride=pitch)
        for k in range(half):
            xp_ref[k, rows, :] = xr[:, k * LANES:(k + 1) * LANES]
            xp_ref[half + k, rows, :] = xi[:, k * LANES:(k + 1) * LANES]
        zr = jnp.concatenate([z_ref[k, rows, :] for k in range(half)], axis=1)
        zi = jnp.concatenate([z_ref[half + k, rows, :] for k in range(half)], axis=1)
        xr, xi = ar * xr - ai * xi + zr, ar * xi + ai * xr + zi
    st_ref[:, :SLAB_STATE] = xr
    st_ref[:, SLAB_STATE:] = xi

    for b in range(nb):
        for k in range(n_state_slab):
            xpb_ref[b * mc:(b + 1) * mc, k * LANES:(k + 1) * LANES] = (
                xp_ref[k, b * pitch:b * pitch + mc, :].astype(BF16))

    y = (jnp.concatenate(y_cols, axis=1)
         + jnp.dot(xpb_ref[...], tv_ref[CHUNK_K:, :], preferred_element_type=F32))
    for b in range(nb):
        for t in range(CHUNK):
            y_ref[b, pl.ds(t, mc, stride=CHUNK), :] = y[b * mc:(b + 1) * mc, t * LANES:(t + 1) * LANES]


def _s5_scan(u4, wmat, tv, al, nb, seq, tb):
    mc = tb // CHUNK
    rows = nb * mc
    n_state = 2 * SLAB_STATE
    kern = functools.partial(_s5_kernel, nb=nb, mc=mc)
    blk = lambda s, j: (s, 0, j, 0)
    per_slab = lambda s, j: (s, 0, 0)
    return pl.pallas_call(
        kern,
        grid=(N_SLAB, seq // tb),
        in_specs=[
            pl.BlockSpec((None, nb, tb, LANES), blk),
            pl.BlockSpec((None, CHUNK_K, n_state), per_slab),
            pl.BlockSpec((None, CHUNK_K + n_state, CHUNK_K), per_slab),
            pl.BlockSpec((None, 2, SLAB_STATE), per_slab),
        ],
        out_specs=pl.BlockSpec((None, nb, tb, LANES), blk),
        out_shape=jax.ShapeDtypeStruct((N_SLAB, nb, seq, LANES), F32),
        scratch_shapes=[
            pltpu.VMEM((rows, CHUNK_K), BF16),
            pltpu.VMEM((rows, n_state), BF16),
            pltpu.VMEM((n_state // LANES, nb * (mc + 4), LANES), F32),
            pltpu.VMEM((n_state // LANES, nb * (mc + 4), LANES), F32),
            pltpu.VMEM((nb, n_state), F32),
        ],
        compiler_params=pltpu.CompilerParams(
            dimension_semantics=("arbitrary", "arbitrary"), vmem_limit_bytes=VMEM_LIMIT),
        name="s5_scan",
    )(u4, wmat, tv, al)


def _s5_out_kernel(y4_ref, gluw_ref, glub_ref, sog_ref, o_ref):
    y = jnp.concatenate([y4_ref[s] for s in range(N_SLAB)], axis=1)
    q = 0.25 * y
    t = jnp.tanh(y * (GELU_C1 * (y * y) + GELU_C0))
    hy = q * t + q
    half_gate = (jnp.dot(hy.astype(BF16), gluw_ref[...], preferred_element_type=F32)
                 + 0.5 * glub_ref[...])
    o_ref[...] = _rmsnorm(hy * jnp.tanh(half_gate) + hy, sog_ref[...]).astype(BF16)


def _s5_out(y4, gluw, glub, sog, tr):
    t = y4.shape[1]
    return pl.pallas_call(
        _s5_out_kernel,
        grid=(t // tr,),
        in_specs=[
            pl.BlockSpec((N_SLAB, tr, LANES), lambda i: (0, i, 0)),
            _const_spec((D_SSM, D_SSM)),
            _const_spec((1, D_SSM)),
            _const_spec((1, D_SSM)),
        ],
        out_specs=pl.BlockSpec((tr, D_SSM), lambda i: (i, 0)),
        out_shape=jax.ShapeDtypeStruct((t, D_SSM), BF16),
        compiler_params=pltpu.CompilerParams(
            dimension_semantics=("parallel",), vmem_limit_bytes=VMEM_LIMIT),
        name="s5_out",
    )(y4, gluw, glub, sog)


CONV_ROWS = 32
CONV_STRIDE = 4
MIX_ROWS = CONV_ROWS * CONV_STRIDE


def _order_token(v):
    t = v[:, :LANES]
    for lo in range(LANES, v.shape[1], LANES):
        t = jnp.maximum(t, v[:, lo:lo + LANES])
    return jnp.max(t, axis=0, keepdims=True)


def _wait_for(x, tokens):
    for t in tokens:
        bits = pltpu.bitcast(t, jnp.uint32)
        bits = lax.shift_right_logical(lax.shift_right_logical(bits, jnp.uint32(16)), jnp.uint32(16))
        x = pltpu.bitcast(pltpu.bitcast(x, jnp.uint32) + bits, F32)
    return x


def _conv_chain(r0, s, ph, after, cw_ref, cb_ref, aext_ref, conv_ref):
    base = HALO - (CONV_WIDTH - 1)
    lanes = slice(s * LANES, (s + 1) * LANES)
    acc = _wait_for(jnp.broadcast_to(cb_ref[:, lanes], (CONV_ROWS, LANES)), after)
    for k in range(CONV_WIDTH):
        rows = pl.ds(r0 + ph + base + k, CONV_ROWS, stride=CONV_STRIDE)
        acc = acc + cw_ref[k:k + 1, lanes] * aext_ref[s, rows, :]
    conv_ref[s, pl.ds(r0 + ph, CONV_ROWS, stride=CONV_STRIDE), :] = acc
    return _order_token(acc)


def _mix_rows(r0, conv_ref, s_ref, lng_ref, lnb_ref, cog_ref, dst_ref):
    rows = slice(r0, r0 + MIX_ROWS)
    c = jnp.concatenate([conv_ref[s, rows, :] for s in range(C_SLAB)], axis=1)
    mu = jnp.mean(c, axis=-1, keepdims=True)
    xc = c - mu
    var = jnp.mean(xc * xc, axis=-1, keepdims=True)
    c = xc * lax.rsqrt(var + EPS) * lng_ref[...] + lnb_ref[...]
    c = c * jax.nn.sigmoid(c)
    a_out = _rmsnorm(c, cog_ref[...])
    dst_ref[rows, :] = jnp.concatenate([a_out.astype(BF16), s_ref[rows, :]], axis=1)


def _mix_ffn2_kernel(x1_ref, a_ref, halo_ref, s_ref, cw_ref, cb_ref, lng_ref, lnb_ref, cog_ref,
                     wout_ref, g2_ref, w1_ref, w3_ref, w2_ref, gf_ref,
                     o_ref, aext_ref, conv_ref, mixed_ref, z_ref, *, ts, nt, n_tiles):
    g = pl.program_id(0)

    @pl.when(g == 0)
    def _():
        mixed_ref[...] = jnp.zeros_like(mixed_ref)

    first = jnp.minimum(g, n_tiles - 1) % nt == 0
    aext_ref[:, 0:HALO, :] = jnp.where(first, 0.0, halo_ref[...])
    aext_ref[:, HALO:, :] = a_ref[...]

    conv_after = []

    def conv_piece(r0, s, ph):
        conv_after[:] = [_conv_chain(r0, s, ph, conv_after, cw_ref, cb_ref, aext_ref, conv_ref)]

    def rows_piece(r0):
        _mix_rows(r0, conv_ref, s_ref, lng_ref, lnb_ref, cog_ref, mixed_ref.at[g % 2])

    mix_pieces = []
    for r0 in range(0, ts, MIX_ROWS):
        for s in range(C_SLAB):
            for ph in range(CONV_STRIDE):
                mix_pieces.append(functools.partial(conv_piece, r0, s, ph))
        mix_pieces.append(functools.partial(rows_piece, r0))
    n_ff = D_FF // MXU_TILE
    done = [0]

    def fill(slot, result):
        upto = (len(mix_pieces) * (slot + 1)) // n_ff
        conv_after.append(_order_token(result))
        for piece in mix_pieces[done[0]:upto]:
            piece()
        done[0] = upto

    col = lambda n: slice(n * MXU_TILE, (n + 1) * MXU_TILE)
    mixed_prev = mixed_ref[(g + 1) % 2]
    x2 = x1_ref[...] + jnp.dot(mixed_prev, wout_ref[...], preferred_element_type=F32)
    h = _rmsnorm(x2, g2_ref[...]).astype(BF16)
    for n in range(n_ff):
        a = jnp.dot(h, w1_ref[:, col(n)], preferred_element_type=F32)
        b = jnp.dot(h, w3_ref[:, col(n)], preferred_element_type=F32)
        zf = a * jax.nn.sigmoid(a) * b
        z_ref[:, col(n)] = zf.astype(BF16)
        fill(n, zf)
    z = z_ref[...]
    x3 = x2 + FFN_RES * jnp.dot(z, w2_ref[...], preferred_element_type=F32)
    o_ref[...] = _rmsnorm(x3, gf_ref[...])


def _mix_ffn2(x1, a, s_out, cw, cb, lng, lnb, cog, wout, g2, w1, w3, w2, gf, nb, seq, ts):
    nt = seq // ts
    n_tiles = nb * nt
    hb = ts // HALO
    cur = lambda g: jnp.minimum(g, n_tiles - 1)
    prev = lambda g: jnp.maximum(g - 1, 0)
    kern = functools.partial(_mix_ffn2_kernel, ts=ts, nt=nt, n_tiles=n_tiles)
    return pl.pallas_call(
        kern,
        grid=(n_tiles + 1,),
        in_specs=[
            pl.BlockSpec((ts, D_MODEL), lambda g: (prev(g), 0)),
            pl.BlockSpec((C_SLAB, ts, LANES), lambda g: (0, cur(g), 0)),
            pl.BlockSpec((C_SLAB, HALO, LANES), lambda g: (0, jnp.maximum(cur(g) * hb - 1, 0), 0)),
            pl.BlockSpec((ts, D_SSM), lambda g: (cur(g), 0)),
            _const_spec((CONV_WIDTH, D_CONV)),
            _const_spec((1, D_CONV)),
            _const_spec((1, D_CONV)),
            _const_spec((1, D_CONV)),
            _const_spec((1, D_CONV)),
            _const_spec((D_MODEL, D_MODEL)),
            _const_spec((1, D_MODEL)),
            _const_spec((D_MODEL, D_FF)),
            _const_spec((D_MODEL, D_FF)),
            _const_spec((D_FF, D_MODEL)),
            _const_spec((1, D_MODEL)),
        ],
        out_specs=pl.BlockSpec((ts, D_MODEL), lambda g: (prev(g), 0)),
        out_shape=jax.ShapeDtypeStruct((nb * seq, D_MODEL), F32),
        scratch_shapes=[
            pltpu.VMEM((C_SLAB, HALO + ts, LANES), F32),
            pltpu.VMEM((C_SLAB, ts, LANES), F32),
            pltpu.VMEM((2, ts, D_MODEL), BF16),
            pltpu.VMEM((ts, D_FF), BF16),
        ],
        compiler_params=pltpu.CompilerParams(
            dimension_semantics=("arbitrary",), vmem_limit_bytes=VMEM_LIMIT),
        name="mix_ffn2",
    )(x1, a, a, s_out, cw, cb, lng, lnb, cog, wout, g2, w1, w3, w2, gf)


def kernel(x, norm_ffn1, ffn1_w1, ffn1_w3, ffn1_w2, norm_mix, w_in, conv_w, conv_b, conv_ln_g,
           conv_ln_b, conv_out_g, ssm_A_re, ssm_A_im, ssm_log_dt, ssm_B_re, ssm_B_im, ssm_C_re,
           ssm_C_im, ssm_D, ssm_glu_w, ssm_glu_b, ssm_out_g, w_out, norm_ffn2, ffn2_w1, ffn2_w3,
           ffn2_w2, norm_final):
    nb, seq, _ = x.shape
    depth = norm_ffn1.shape[0]
    assert depth == 1, "the final norm is fused into the last FFN; only depth 1 is laid out"
    tm = 512
    tb = 256
    tr = 4096
    ts = 512
    assert seq % tb == 0 and seq % ts == 0 and (nb * seq) % tm == 0 and (nb * seq) % tr == 0
    assert tb % (CHUNK * 16) == 0 and ts % HALO == 0 and ts % MIX_ROWS == 0

    row = lambda v: v.reshape(1, -1).astype(F32)
    x2d = x.reshape(nb * seq, D_MODEL)
    for l in range(depth):
        x1, a, u4 = _ffn1_proj(
            x2d, row(norm_ffn1[l]), ffn1_w1[l].astype(BF16), ffn1_w3[l].astype(BF16),
            ffn1_w2[l].astype(BF16), row(norm_mix[l]), w_in[l].astype(BF16), tm)
        wc, vc, al = _s5_compact(
            ssm_A_re[l], ssm_A_im[l], ssm_log_dt[l], ssm_B_re[l], ssm_B_im[l],
            ssm_C_re[l], ssm_C_im[l])
        wmat, tv = _s5_embed(wc, vc, ssm_D[l].reshape(N_SLAB, 1, LANES))
        y4 = _s5_scan(u4.reshape(N_SLAB, nb, seq, LANES), wmat, tv, al, nb, seq, tb)
        s_out = _s5_out(y4.reshape(N_SLAB, nb * seq, LANES), ssm_glu_w[l].astype(BF16),
                        row(ssm_glu_b[l]), row(ssm_out_g[l]), tr)
        x2d = _mix_ffn2(
            x1, a, s_out, conv_w[l], row(conv_b[l]), row(conv_ln_g[l]), row(conv_ln_b[l]),
            row(conv_out_g[l]), w_out[l].astype(BF16), row(norm_ffn2[l]),
            ffn2_w1[l].astype(BF16), ffn2_w3[l].astype(BF16), ffn2_w2[l].astype(BF16),
            row(norm_final), nb, seq, ts)
    return x2d.reshape(nb, seq, D_MODEL)
```

```python
import functools
import math

import jax
import jax.numpy as jnp
from jax import lax
from jax.experimental import pallas as pl
from jax.experimental.pallas import tpu as pltpu

D_MODEL = 1024
D_CONV = 512
D_SSM = 512
CONV_WIDTH = 31
SSM_GROUP = 16
SSM_GROUPS = 32
SSM_STATE = 64
D_FF = 2816
D_IN = 2 * D_CONV + D_SSM
FFN_RES = 0.5
EPS = 1e-6
GELU_C0 = math.sqrt(2.0 / math.pi)
GELU_C1 = 0.044715 * GELU_C0

LANES = 128
MXU_TILE = 256
SLAB_GROUPS = LANES // SSM_GROUP
N_SLAB = D_SSM // LANES
C_SLAB = D_CONV // LANES
SLAB_STATE = SLAB_GROUPS * SSM_STATE
CHUNK = 4
CHUNK_K = CHUNK * LANES
HALO = 32

VMEM_LIMIT = 56 * 1024 * 1024

F32 = jnp.float32
BF16 = jnp.bfloat16


def _sigmoid(x):
    return 0.5 * jnp.tanh(0.5 * x) + 0.5


def _rmsnorm(x, g):
    ms = jnp.mean(x * x, axis=-1, keepdims=True)
    return x * lax.rsqrt(ms + EPS) * g


def _swiglu_residual(x, g, w1_ref, w3_ref, w2_ref):
    h = _rmsnorm(x, g).astype(BF16)
    z = []
    for n in range(D_FF // MXU_TILE):
        cols = slice(n * MXU_TILE, (n + 1) * MXU_TILE)
        ha = 0.5 * jnp.dot(h, w1_ref[:, cols], preferred_element_type=F32)
        b = jnp.dot(h, w3_ref[:, cols], preferred_element_type=F32)
        z.append(((ha * jnp.tanh(ha) + ha) * b).astype(BF16))
    o = jnp.dot(jnp.concatenate(z, axis=1), w2_ref[...], preferred_element_type=F32)
    return x + FFN_RES * o


def _const_spec(shape):
    nd = len(shape)
    return pl.BlockSpec(shape, lambda *_: (0,) * nd, pipeline_mode=pl.Buffered(1))


def _ffn1_proj_kernel(x_ref, g1_ref, w1_ref, w3_ref, w2_ref, gm_ref, win_ref,
                      x1_ref, a_ref, u_ref):
    x1 = _swiglu_residual(x_ref[...], g1_ref[...], w1_ref, w3_ref, w2_ref)
    x1_ref[...] = x1
    h2 = _rmsnorm(x1, gm_ref[...]).astype(BF16)
    proj = jnp.dot(h2, win_ref[...], preferred_element_type=F32)
    for s in range(C_SLAB):
        lo = s * LANES
        a_val = proj[:, lo:lo + LANES]
        a_gate = proj[:, D_CONV + lo:D_CONV + lo + LANES]
        a_ref[s] = a_val * _sigmoid(a_gate)
    for s in range(N_SLAB):
        lo = 2 * D_CONV + s * LANES
        u_ref[s] = proj[:, lo:lo + LANES]


def _ffn1_proj(x2d, g1, w1, w3, w2, gm, win, tm):
    t = x2d.shape[0]
    row = lambda i: (i, 0)
    return pl.pallas_call(
        _ffn1_proj_kernel,
        grid=(t // tm,),
        in_specs=[
            pl.BlockSpec((tm, D_MODEL), row),
            _const_spec((1, D_MODEL)),
            _const_spec((D_MODEL, D_FF)),
            _const_spec((D_MODEL, D_FF)),
            _const_spec((D_FF, D_MODEL)),
            _const_spec((1, D_MODEL)),
            _const_spec((D_MODEL, D_IN)),
        ],
        out_specs=[
            pl.BlockSpec((tm, D_MODEL), row),
            pl.BlockSpec((C_SLAB, tm, LANES), lambda i: (0, i, 0)),
            pl.BlockSpec((N_SLAB, tm, LANES), lambda i: (0, i, 0)),
        ],
        out_shape=[
            jax.ShapeDtypeStruct((t, D_MODEL), F32),
            jax.ShapeDtypeStruct((C_SLAB, t, LANES), F32),
            jax.ShapeDtypeStruct((N_SLAB, t, LANES), F32),
        ],
        compiler_params=pltpu.CompilerParams(
            dimension_semantics=("parallel",), vmem_limit_bytes=VMEM_LIMIT),
        name="ffn1_proj",
    )(x2d, g1, w1, w3, w2, gm, win)


def _s5_compact(a_re, a_im, log_dt, b_re, b_im, c_re, c_im):
    L = CHUNK
    dt = jnp.exp(log_dt)[:, None]
    zr, zi = a_re * dt, a_im * dt
    mag = jnp.exp(zr)
    abar_r, abar_i = mag * jnp.cos(zi), mag * jnp.sin(zi)
    den = a_re * a_re + a_im * a_im
    nr = abar_r - 1.0
    coef_r = (nr * a_re + abar_i * a_im) / den
    coef_i = (abar_i * a_re - nr * a_im) / den
    bb_r = coef_r[..., None] * b_re - coef_i[..., None] * b_im
    bb_i = coef_r[..., None] * b_im + coef_i[..., None] * b_re
    n = jnp.arange(L + 1, dtype=F32)[:, None, None]
    pmag = jnp.exp(n * zr)
    pw_r, pw_i = pmag * jnp.cos(n * zi), pmag * jnp.sin(n * zi)
    cp_r = c_re[None] * pw_r[:, :, None, :] - c_im[None] * pw_i[:, :, None, :]
    cp_i = c_re[None] * pw_i[:, :, None, :] + c_im[None] * pw_r[:, :, None, :]
    vc = jnp.concatenate([cp_r, -cp_i], axis=-1)
    nrev = (L - 1) - n[:L]
    rmag = jnp.exp(nrev * zr)
    rv_r = (rmag * jnp.cos(nrev * zi))[:, :, None, :]
    rv_i = (rmag * jnp.sin(nrev * zi))[:, :, None, :]
    bt_r, bt_i = bb_r.transpose(0, 2, 1)[None], bb_i.transpose(0, 2, 1)[None]
    wc = jnp.concatenate([rv_r * bt_r - rv_i * bt_i, rv_r * bt_i + rv_i * bt_r], axis=-1)

    def per_slab(v):
        v = v.reshape(v.shape[0], N_SLAB, LANES, 2 * SSM_STATE)
        return v.transpose(1, 0, 2, 3).reshape(N_SLAB, -1, 2 * SSM_STATE)

    al = jnp.stack([pw_r[L].reshape(N_SLAB, SLAB_STATE), pw_i[L].reshape(N_SLAB, SLAB_STATE)], axis=1)
    return per_slab(wc), per_slab(vc), al


def _s5_embed_kernel(wc_ref, vc_ref, d_ref, w_ref, tv_ref):
    L = CHUNK

    def embed(c):
        rows = c.shape[0]
        row_group = (lax.broadcasted_iota(jnp.int32, (rows, LANES), 0) // SSM_GROUP) % SLAB_GROUPS
        lane_half = lax.broadcasted_iota(jnp.int32, (rows, LANES), 1) // SSM_STATE
        pieces = []
        for part in range(2):
            x = c[:, part * SSM_STATE:(part + 1) * SSM_STATE]
            xx = jnp.concatenate([x] * (LANES // SSM_STATE), axis=1)
            for k in range(SLAB_STATE // LANES):
                own = row_group == (LANES // SSM_STATE) * k + lane_half
                pieces.append(jnp.where(own, xx, 0.0))
        return jnp.concatenate(pieces, axis=1)

    w = embed(wc_ref[...])
    vt = embed(vc_ref[...])
    w_ref[...] = w.astype(BF16)

    w0 = w[(L - 1) * LANES:, :]
    nt = (((1,), (1,)), ((), ()))
    ri = lax.broadcasted_iota(jnp.int32, (LANES, LANES), 0)
    ci = lax.broadcasted_iota(jnp.int32, (LANES, LANES), 1)
    zero = jnp.zeros((LANES, LANES), BF16)
    for n in range(L):
        k_n = lax.dot_general(w0, vt[n * LANES:(n + 1) * LANES, :], nt,
                              precision=lax.Precision.HIGHEST, preferred_element_type=F32)
        if n == 0:
            k_n = k_n + jnp.where(ri == ci, d_ref[...], 0.0)
        k_n = k_n.astype(BF16)
        for s in range(L - n):
            t = s + n
            tv_ref[s * LANES:(s + 1) * LANES, t * LANES:(t + 1) * LANES] = k_n
    for s in range(1, L):
        for t in range(s):
            tv_ref[s * LANES:(s + 1) * LANES, t * LANES:(t + 1) * LANES] = zero
    tv_ref[L * LANES:, :] = vt[LANES:, :].T.astype(BF16)


def _s5_embed(wc, vc, d4):
    n_state = 2 * SLAB_STATE
    per_slab = lambda s: (s, 0, 0)
    return pl.pallas_call(
        _s5_embed_kernel,
        grid=(N_SLAB,),
        in_specs=[
            pl.BlockSpec((None, CHUNK_K, 2 * SSM_STATE), per_slab),
            pl.BlockSpec((None, CHUNK_K + LANES, 2 * SSM_STATE), per_slab),
            pl.BlockSpec((None, 1, LANES), per_slab),
        ],
        out_specs=[
            pl.BlockSpec((None, CHUNK_K, n_state), per_slab),
            pl.BlockSpec((None, CHUNK_K + n_state, CHUNK_K), per_slab),
        ],
        out_shape=[
            jax.ShapeDtypeStruct((N_SLAB, CHUNK_K, n_state), BF16),
            jax.ShapeDtypeStruct((N_SLAB, CHUNK_K + n_state, CHUNK_K), BF16),
        ],
        compiler_params=pltpu.CompilerParams(
            dimension_semantics=("parallel",), vmem_limit_bytes=VMEM_LIMIT),
        name="s5_embed",
    )(wc, vc, d4)


def _s5_kernel(u_ref, w_ref, tv_ref, al_ref, y_ref, lhs_ref, xpb_ref, z_ref, xp_ref, st_ref,
               *, nb, mc):
    n_state_slab = 2 * SLAB_STATE // LANES
    half = n_state_slab // 2
    pitch = mc + 4

    @pl.when(pl.program_id(1) == 0)
    def _():
        st_ref[...] = jnp.zeros_like(st_ref)

    for b in range(nb):
        for s in range(CHUNK):
            piece = u_ref[b, pl.ds(s, mc, stride=CHUNK), :]
            lhs_ref[b * mc:(b + 1) * mc, s * LANES:(s + 1) * LANES] = piece.astype(BF16)

    z = jnp.dot(lhs_ref[...], w_ref[...], preferred_element_type=F32)
    for b in range(nb):
        for k in range(n_state_slab):
            z_ref[k, b * pitch:b * pitch + mc, :] = z[b * mc:(b + 1) * mc, k * LANES:(k + 1) * LANES]

    y_cols = []
    for n in range(CHUNK_K // MXU_TILE):
        kk = (n + 1) * MXU_TILE
        y_cols.append(jnp.dot(lhs_ref[:, :kk], tv_ref[:kk, n * MXU_TILE:(n + 1) * MXU_TILE],
                              preferred_element_type=F32))

    ar = al_ref[0:1, :]
    ai = al_ref[1:2, :]
    xr = st_ref[:, :SLAB_STATE]
    xi = st_ref[:, SLAB_STATE:]
    for m in range(mc):
        rows = pl.ds(m, nb, stride=pitch)
        for k in range(half):
            xp_ref[k, rows, :] = xr[:, k * LANES:(k + 1) * LANES]
            xp_ref[half + k, rows, :] = xi[:, k * LANES:(k + 1) * LANES]
        zr = jnp.concatenate([z_ref[k, rows, :] for k in range(half)], axis=1)
        zi = jnp.concatenate([z_ref[half + k, rows, :] for k in range(half)], axis=1)
        xr, xi = ar * xr - ai * xi + zr, ar * xi + ai * xr + zi
    st_ref[:, :SLAB_STATE] = xr
    st_ref[:, SLAB_STATE:] = xi

    for b in range(nb):
        for k in range(n_state_slab):
            xpb_ref[b * mc:(b + 1) * mc, k * LANES:(k + 1) * LANES] = (
                xp_ref[k, b * pitch:b * pitch + mc, :].astype(BF16))

    y = (jnp.concatenate(y_cols, axis=1)
         + jnp.dot(xpb_ref[...], tv_ref[CHUNK_K:, :], preferred_element_type=F32))
    for b in range(nb):
        for t in range(CHUNK):
            y_ref[b, pl.ds(t, mc, stride=CHUNK), :] = y[b * mc:(b + 1) * mc, t * LANES:(t + 1) * LANES]


def _s5_scan(u4, wmat, tv, al, nb, seq, tb):
    mc = tb // CHUNK
    rows = nb * mc
    n_state = 2 * SLAB_STATE
    kern = functools.partial(_s5_kernel, nb=nb, mc=mc)
    blk = lambda s, j: (s, 0, j, 0)
    per_slab = lambda s, j: (s, 0, 0)
    return pl.pallas_call(
        kern,
        grid=(N_SLAB, seq // tb),
        in_specs=[
            pl.BlockSpec((None, nb, tb, LANES), blk),
            pl.BlockSpec((None, CHUNK_K, n_state), per_slab),
            pl.BlockSpec((None, CHUNK_K + n_state, CHUNK_K), per_slab),
            pl.BlockSpec((None, 2, SLAB_STATE), per_slab),
        ],
        out_specs=pl.BlockSpec((None, nb, tb, LANES), blk),
        out_shape=jax.ShapeDtypeStruct((N_SLAB, nb, seq, LANES), F32),
        scratch_shapes=[
            pltpu.VMEM((rows, CHUNK_K), BF16),
            pltpu.VMEM((rows, n_state), BF16),
            pltpu.VMEM((n_state // LANES, nb * (mc + 4), LANES), F32),
            pltpu.VMEM((n_state // LANES, nb * (mc + 4), LANES), F32),
            pltpu.VMEM((nb, n_state), F32),
        ],
        compiler_params=pltpu.CompilerParams(
            dimension_semantics=("arbitrary", "arbitrary"), vmem_limit_bytes=VMEM_LIMIT),
        name="s5_scan",
    )(u4, wmat, tv, al)


def _s5_out_kernel(y4_ref, gluw_ref, glub_ref, sog_ref, o_ref):
    y = jnp.concatenate([y4_ref[s] for s in range(N_SLAB)], axis=1)
    q = 0.25 * y
    t = jnp.tanh(y * (GELU_C1 * (y * y) + GELU_C0))
    hy = q * t + q
    half_gate = (jnp.dot(hy.astype(BF16), gluw_ref[...], preferred_element_type=F32)
                 + 0.5 * glub_ref[...])
    o_ref[...] = _rmsnorm(hy * jnp.tanh(half_gate) + hy, sog_ref[...]).astype(BF16)


def _s5_out(y4, gluw, glub, sog, tr):
    t = y4.shape[1]
    return pl.pallas_call(
        _s5_out_kernel,
        grid=(t // tr,),
        in_specs=[
            pl.BlockSpec((N_SLAB, tr, LANES), lambda i: (0, i, 0)),
            _const_spec((D_SSM, D_SSM)),
            _const_spec((1, D_SSM)),
            _const_spec((1, D_SSM)),
        ],
        out_specs=pl.BlockSpec((tr, D_SSM), lambda i: (i, 0)),
        out_shape=jax.ShapeDtypeStruct((t, D_SSM), BF16),
        compiler_params=pltpu.CompilerParams(
            dimension_semantics=("parallel",), vmem_limit_bytes=VMEM_LIMIT),
        name="s5_out",
    )(y4, gluw, glub, sog)


CONV_ROWS = 32
CONV_STRIDE = 4
MIX_ROWS = CONV_ROWS * CONV_STRIDE


def _order_token(v):
    t = v[:, :LANES]
    for lo in range(LANES, v.shape[1], LANES):
        t = jnp.maximum(t, v[:, lo:lo + LANES])
    return jnp.max(t, axis=0, keepdims=True)


def _wait_for(x, tokens):
    for t in tokens:
        bits = pltpu.bitcast(t, jnp.uint32)
        bits = lax.shift_right_logical(lax.shift_right_logical(bits, jnp.uint32(16)), jnp.uint32(16))
        x = pltpu.bitcast(pltpu.bitcast(x, jnp.uint32) + bits, F32)
    return x


def _conv_chain(r0, s, ph, after, cw_ref, cb_ref, aext_ref, conv_ref):
    base = HALO - (CONV_WIDTH - 1)
    lanes = slice(s * LANES, (s + 1) * LANES)
    acc = _wait_for(jnp.broadcast_to(cb_ref[:, lanes], (CONV_ROWS, LANES)), after)
    for k in range(CONV_WIDTH):
        rows = pl.ds(r0 + ph + base + k, CONV_ROWS, stride=CONV_STRIDE)
        acc = acc + cw_ref[k:k + 1, lanes] * aext_ref[s, rows, :]
    conv_ref[s, pl.ds(r0 + ph, CONV_ROWS, stride=CONV_STRIDE), :] = acc
    return _order_token(acc)


def _mix_rows(r0, conv_ref, s_ref, lng_ref, lnb_ref, cog_ref, dst_ref):
    rows = slice(r0, r0 + MIX_ROWS)
    c = jnp.concatenate([conv_ref[s, rows, :] for s in range(C_SLAB)], axis=1)
    mu = jnp.mean(c, axis=-1, keepdims=True)
    xc = c - mu
    var = jnp.mean(xc * xc, axis=-1, keepdims=True)
    c = xc * lax.rsqrt(var + EPS) * lng_ref[...] + lnb_ref[...]
    hc = 0.5 * c
    c = hc * jnp.tanh(hc) + hc
    a_out = _rmsnorm(c, cog_ref[...])
    dst_ref[rows, :] = jnp.concatenate([a_out.astype(BF16), s_ref[rows, :]], axis=1)


def _mix_ffn2_kernel(x1_ref, a_ref, halo_ref, s_ref, cw_ref, cb_ref, lng_ref, lnb_ref, cog_ref,
                     wout_ref, g2_ref, w1_ref, w3_ref, w2_ref, gf_ref,
                     o_ref, aext_ref, conv_ref, mixed_ref, z_ref, *, ts, nt, n_tiles):
    g = pl.program_id(0)

    @pl.when(g == 0)
    def _():
        mixed_ref[...] = jnp.zeros_like(mixed_ref)

    first = jnp.minimum(g, n_tiles - 1) % nt == 0
    aext_ref[:, 0:HALO, :] = jnp.where(first, 0.0, halo_ref[...])
    aext_ref[:, HALO:, :] = a_ref[...]

    conv_after = []

    def conv_piece(r0, s, ph):
        conv_after[:] = [_conv_chain(r0, s, ph, conv_after, cw_ref, cb_ref, aext_ref, conv_ref)]

    def rows_piece(r0):
        _mix_rows(r0, conv_ref, s_ref, lng_ref, lnb_ref, cog_ref, mixed_ref.at[g % 2])

    mix_pieces = []
    for r0 in range(0, ts, MIX_ROWS):
        for s in range(C_SLAB):
            for ph in range(CONV_STRIDE):
                mix_pieces.append(functools.partial(conv_piece, r0, s, ph))
        mix_pieces.append(functools.partial(rows_piece, r0))
    n_ff = D_FF // MXU_TILE
    done = [0]

    def fill(slot, result):
        upto = (len(mix_pieces) * (slot + 1)) // n_ff
        conv_after.append(_order_token(result))
        for piece in mix_pieces[done[0]:upto]:
            piece()
        done[0] = upto

    col = lambda n: slice(n * MXU_TILE, (n + 1) * MXU_TILE)
    mixed_prev = mixed_ref[(g + 1) % 2]
    x2 = x1_ref[...] + jnp.dot(mixed_prev, wout_ref[...], preferred_element_type=F32)
    h = _rmsnorm(x2, g2_ref[...]).astype(BF16)
    for n in range(n_ff):
        a = jnp.dot(h, w1_ref[:, col(n)], preferred_element_type=F32)
        b = jnp.dot(h, w3_ref[:, col(n)], preferred_element_type=F32)
        ha = 0.5 * a
        zf = (ha * jnp.tanh(ha) + ha) * b
        z_ref[:, col(n)] = zf.astype(BF16)
        fill(n, zf)
    z = z_ref[...]
    x3 = x2 + FFN_RES * jnp.dot(z, w2_ref[...], preferred_element_type=F32)
    o_ref[...] = _rmsnorm(x3, gf_ref[...])


def _mix_ffn2(x1, a, s_out, cw, cb, lng, lnb, cog, wout, g2, w1, w3, w2, gf, nb, seq, ts):
    nt = seq // ts
    n_tiles = nb * nt
    hb = ts // HALO
    cur = lambda g: jnp.minimum(g, n_tiles - 1)
    prev = lambda g: jnp.maximum(g - 1, 0)
    kern = functools.partial(_mix_ffn2_kernel, ts=ts, nt=nt, n_tiles=n_tiles)
    return pl.pallas_call(
        kern,
        grid=(n_tiles + 1,),
        in_specs=[
            pl.BlockSpec((ts, D_MODEL), lambda g: (prev(g), 0)),
            pl.BlockSpec((C_SLAB, ts, LANES), lambda g: (0, cur(g), 0)),
            pl.BlockSpec((C_SLAB, HALO, LANES), lambda g: (0, jnp.maximum(cur(g) * hb - 1, 0), 0)),
            pl.BlockSpec((ts, D_SSM), lambda g: (cur(g), 0)),
            _const_spec((CONV_WIDTH, D_CONV)),
            _const_spec((1, D_CONV)),
            _const_spec((1, D_CONV)),
            _const_spec((1, D_CONV)),
            _const_spec((1, D_CONV)),
            _const_spec((D_MODEL, D_MODEL)),
            _const_spec((1, D_MODEL)),
            _const_spec((D_MODEL, D_FF)),
            _const_spec((D_MODEL, D_FF)),
            _const_spec((D_FF, D_MODEL)),
            _const_spec((1, D_MODEL)),
        ],
        out_specs=pl.BlockSpec((ts, D_MODEL), lambda g: (prev(g), 0)),
        out_shape=jax.ShapeDtypeStruct((nb * seq, D_MODEL), F32),
        scratch_shapes=[
            pltpu.VMEM((C_SLAB, HALO + ts, LANES), F32),
            pltpu.VMEM((C_SLAB, ts, LANES), F32),
            pltpu.VMEM((2, ts, D_MODEL), BF16),
            pltpu.VMEM((ts, D_FF), BF16),
        ],
        compiler_params=pltpu.CompilerParams(
            dimension_semantics=("arbitrary",), vmem_limit_bytes=VMEM_LIMIT),
        name="mix_ffn2",
    )(x1, a, a, s_out, cw, cb, lng, lnb, cog, wout, g2, w1, w3, w2, gf)


def kernel(x, norm_ffn1, ffn1_w1, ffn1_w3, ffn1_w2, norm_mix, w_in, conv_w, conv_b, conv_ln_g,
           conv_ln_b, conv_out_g, ssm_A_re, ssm_A_im, ssm_log_dt, ssm_B_re, ssm_B_im, ssm_C_re,
           ssm_C_im, ssm_D, ssm_glu_w, ssm_glu_b, ssm_out_g, w_out, norm_ffn2, ffn2_w1, ffn2_w3,
           ffn2_w2, norm_final):
    nb, seq, _ = x.shape
    depth = norm_ffn1.shape[0]
    assert depth == 1, "the final norm is fused into the last FFN; only depth 1 is laid out"
    tm = 512
    tb = 256
    tr = 4096
    ts = 512
    assert seq % tb == 0 and seq % ts == 0 and (nb * seq) % tm == 0 and (nb * seq) % tr == 0
    assert tb % (CHUNK * 16) == 0 and ts % HALO == 0 and ts % MIX_ROWS == 0

    row = lambda v: v.reshape(1, -1).astype(F32)
    x2d = x.reshape(nb * seq, D_MODEL)
    for l in range(depth):
        x1, a, u4 = _ffn1_proj(
            x2d, row(norm_ffn1[l]), ffn1_w1[l].astype(BF16), ffn1_w3[l].astype(BF16),
            ffn1_w2[l].astype(BF16), row(norm_mix[l]), w_in[l].astype(BF16), tm)
        wc, vc, al = _s5_compact(
            ssm_A_re[l], ssm_A_im[l], ssm_log_dt[l], ssm_B_re[l], ssm_B_im[l],
            ssm_C_re[l], ssm_C_im[l])
        wmat, tv = _s5_embed(wc, vc, ssm_D[l].reshape(N_SLAB, 1, LANES))
        y4 = _s5_scan(u4.reshape(N_SLAB, nb, seq, LANES), wmat, tv, al, nb, seq, tb)
        s_out = _s5_out(y4.reshape(N_SLAB, nb * seq, LANES), ssm_glu_w[l].astype(BF16),
                        row(ssm_glu_b[l]), row(ssm_out_g[l]), tr)
        x2d = _mix_ffn2(
            x1, a, s_out, conv_w[l], row(conv_b[l]), row(conv_ln_g[l]), row(conv_ln_b[l]),
            row(conv_out_g[l]), w_out[l].astype(BF16), row(norm_ffn2[l]),
            ffn2_w1[l].astype(BF16), ffn2_w3[l].astype(BF16), ffn2_w2[l].astype(BF16),
            row(norm_final), nb, seq, ts)
    return x2d.reshape(nb, seq, D_MODEL)
```
